```python
import math
import jax, jax.numpy as jnp
from jax import lax
import numpy as np

D_MODEL = 1024
BATCH = 8
SEQ = 2048
DEPTH = 1
DEC_BATCH = 128
DEC_SEQ = 4
PAST_LEN = 16384
PAGE_SIZE = 128

N_META = 16
HG_HEADS = 8
HG_KDIM = 128
HG_VDIM = 128
HG_QK = HG_HEADS * HG_KDIM
HG_V = HG_HEADS * HG_VDIM
HG_CHUNK = 16
SSM_W = 1024
SSM_GROUP = 16
SSM_G = SSM_W // SSM_GROUP
SSM_P = 64
SSM_MIN_RE = -1e-4
STEP_MIN = 0.001
STEP_MAX = 0.1
IN_COLS = 2 * HG_QK + 2 * HG_V + SSM_W + 2 * D_MODEL
PEER_HEADS = 8
PEER_NKEYS = 128
PEER_EXPERTS = PEER_NKEYS * PEER_NKEYS
PEER_DK = 256
PEER_TOPK = 16
PEER_BLOCK = 128
EPS = 1e-6

kernel_name = "hgrn2_s5_peer_hybrid_step"


def rmsnorm(x, w):
    xf = x.astype(jnp.float32)
    y = xf * lax.rsqrt(jnp.mean(xf * xf, axis=-1, keepdims=True) + EPS)
    return (y * w.astype(jnp.float32)).astype(x.dtype)


def hgrn2_scan(q, k, v, lg, s0):
    n, l, h, kd = q.shape
    vd = v.shape[-1]
    c = math.gcd(l, HG_CHUNK)
    nc = l // c

    def to_chunks(a):
        return jnp.moveaxis(a.reshape(n, nc, c, *a.shape[2:]), 1, 0)

    causal = jnp.tril(jnp.ones((c, c), dtype=bool))[None, :, :, None, None]

    def step(s, blk):
        qc, kc, vc, lgc = blk
        b = jnp.cumsum(lgc, axis=1)
        blast = b[:, -1]
        o_inter = jnp.einsum('nthk,nhkv->nthv', qc * jnp.exp(b), s)
        decay = jnp.exp(jnp.where(causal, b[:, :, None] - b[:, None, :], -jnp.inf))
        att = jnp.einsum('nthk,nshk,ntshk->nhts', qc, kc, decay)
        o_intra = jnp.einsum('nhts,nshv->nthv', att, vc)
        s_new = jnp.exp(blast)[..., None] * s + jnp.einsum(
            'nshk,nshv->nhkv', kc * jnp.exp(blast[:, None] - b), vc)
        return s_new, o_inter + o_intra

    s_fin, o = lax.scan(step, s0, (to_chunks(q), to_chunks(k), to_chunks(v), to_chunks(lg)))
    o = jnp.moveaxis(o, 0, 1).reshape(n, l, h, vd)
    return o, s_fin


def s5_scan(u, x0_re, x0_im, a_re, a_im, log_step, b_re, b_im, c_re, c_im, d_skip):
    n, l, _ = u.shape
    f32 = jnp.float32
    ug = u.astype(f32).reshape(n, l, SSM_G, SSM_GROUP)
    lam = lax.complex(jnp.minimum(a_re.astype(f32), SSM_MIN_RE), a_im.astype(f32))
    delta = jnp.exp(log_step.astype(f32))[:, None]
    a_bar = jnp.exp(lam * delta)
    b_bar = ((a_bar - 1.0) / lam)[..., None] * lax.complex(b_re.astype(f32), b_im.astype(f32))
    bu = jnp.einsum('gpc,nlgc->nlgp', b_bar, ug.astype(jnp.complex64))
    x0 = lax.complex(x0_re.astype(f32), x0_im.astype(f32))
    bu = bu.at[:, 0].add(a_bar * x0)
    a_seq = jnp.broadcast_to(a_bar, bu.shape)

    def combine(e1, e2):
        a1, b1 = e1
        a2, b2 = e2
        return a1 * a2, a2 * b1 + b2

    _, xs = lax.associative_scan(combine, (a_seq, bu), axis=1)
    cc = lax.complex(c_re.astype(f32), c_im.astype(f32))
    y = jnp.real(jnp.einsum('gcp,nlgp->nlgc', cc, xs)) + d_skip.astype(f32).reshape(SSM_G, SSM_GROUP) * ug
    x_last = xs[:, -1]
    return y.reshape(n, l, SSM_W), jnp.real(x_last), jnp.imag(x_last)


def peer(h, wq, k1, k2, u_tab, v_tab):
    n, l, d = h.shape
    f32 = jnp.float32
    t = n * l
    nb = -(-t // PEER_BLOCK)
    hf = jnp.pad(h.reshape(t, d), ((0, nb * PEER_BLOCK - t), (0, 0))).reshape(nb, PEER_BLOCK, d)

    def block(xb):
        qry = (xb @ wq).astype(f32).reshape(PEER_BLOCK, PEER_HEADS, 2, PEER_DK // 2)
        s1 = jnp.einsum('thd,hnd->thn', qry[:, :, 0], k1.astype(f32))
        s2 = jnp.einsum('thd,hnd->thn', qry[:, :, 1], k2.astype(f32))
        v1, i1 = lax.top_k(s1, PEER_TOPK)
        v2, i2 = lax.top_k(s2, PEER_TOPK)
        cand = (v1[..., :, None] + v2[..., None, :]).reshape(PEER_BLOCK, PEER_HEADS, PEER_TOPK * PEER_TOPK)
        sc, pos = lax.top_k(cand, PEER_TOPK)
        e1 = jnp.take_along_axis(i1, pos // PEER_TOPK, axis=-1)
        e2 = jnp.take_along_axis(i2, pos % PEER_TOPK, axis=-1)
        expert = e1 * PEER_NKEYS + e2
        gate = jax.nn.softmax(sc, axis=-1)
        act = jax.nn.gelu(jnp.einsum('thkd,td->thk', u_tab[expert].astype(f32), xb.astype(f32)),
                          approximate=False)
        return jnp.einsum('thk,thkd->td', gate * act, v_tab[expert].astype(f32)).astype(h.dtype)

    out = lax.map(block, hf).reshape(nb * PEER_BLOCK, d)[:t]
    return out.reshape(n, l, d)


def layer(x, s_hg, ssm_re, ssm_im, lb, norm1_w, w_in, g_norm_w, ssm_a_re, ssm_a_im, ssm_log_step,
          ssm_b_re, ssm_b_im, ssm_c_re, ssm_c_im, ssm_d, w_glu, b_glu, w_branch_a, w_branch_b, w_out,
          norm2_w, peer_wq, peer_k1, peer_k2, peer_u, peer_v):
    dt = x.dtype
    f32 = jnp.float32
    n, l, _ = x.shape
    h = rmsnorm(x, norm1_w)
    proj = h @ w_in
    sizes = (HG_QK, HG_QK, HG_V, HG_V, SSM_W, D_MODEL, D_MODEL)
    offs = np.cumsum(sizes)[:-1].tolist()
    q, f, i, g, u, ga, gb = jnp.split(proj, offs, axis=-1)
    qh = jax.nn.silu(q.astype(f32)).reshape(n, l, HG_HEADS, HG_KDIM)
    fg = (lb + (1.0 - lb) * jax.nn.sigmoid(f.astype(f32))).reshape(n, l, HG_HEADS, HG_KDIM)
    kh = 1.0 - fg
    lg = jnp.log(fg)
    vh = i.astype(f32).reshape(n, l, HG_HEADS, HG_VDIM)
    o, s_new = hgrn2_scan(qh, kh, vh, lg, s_hg.astype(f32))
    o = rmsnorm(o, g_norm_w) * jax.nn.silu(g.astype(f32)).reshape(n, l, HG_HEADS, HG_VDIM)
    br_a = o.reshape(n, l, HG_V).astype(dt) @ w_branch_a
    y, x_re, x_im = s5_scan(u, ssm_re, ssm_im, ssm_a_re, ssm_a_im, ssm_log_step,
                            ssm_b_re, ssm_b_im, ssm_c_re, ssm_c_im, ssm_d)
    y = jax.nn.gelu(y, approximate=False)
    y = y * jax.nn.sigmoid(y @ w_glu.astype(f32) + b_glu.astype(f32))
    br_b = y.astype(dt) @ w_branch_b
    mixed = jax.nn.sigmoid(ga) * br_a + jax.nn.sigmoid(gb) * br_b
    x = x + mixed @ w_out
    x = x + peer(rmsnorm(x, norm2_w), peer_wq, peer_k1, peer_k2, peer_u, peer_v)
    sd = s_hg.dtype
    return x, s_new.astype(sd), x_re.astype(ssm_re.dtype), x_im.astype(ssm_im.dtype)


def setup_inputs(seed: int = 0) -> dict:
    key = jax.random.key(seed)
    ks = jax.random.split(key, 32)
    f32 = jnp.float32
    nrm = lambda k, shape, s: jax.random.normal(k, shape, f32) * s
    inp = {}
    inp['x_prompt'] = nrm(ks[0], (BATCH, SEQ, D_MODEL), 1.0)
    inp['x_sample'] = nrm(ks[1], (DEC_BATCH, DEC_SEQ, D_MODEL), 1.0)
    inp['state_hgrn'] = nrm(ks[2], (DEPTH, DEC_BATCH, HG_HEADS, HG_KDIM, HG_VDIM), 0.5)
    inp['state_ssm_re'] = nrm(ks[3], (DEPTH, DEC_BATCH, SSM_G, SSM_P), 0.3)
    inp['state_ssm_im'] = nrm(ks[4], (DEPTH, DEC_BATCH, SSM_G, SSM_P), 0.3)
    inp['meta_tokens'] = nrm(ks[5], (N_META, D_MODEL), 1.0)
    inp['lower_bounds'] = nrm(ks[6], (DEPTH + 1, HG_QK), 0.1)
    inp['norm1_w'] = 1.0 + nrm(ks[7], (DEPTH, D_MODEL), 0.01)
    inp['w_in'] = nrm(ks[8], (DEPTH, D_MODEL, IN_COLS), D_MODEL ** -0.5)
    inp['g_norm_w'] = 1.0 + nrm(ks[9], (DEPTH, HG_VDIM), 0.01)
    inp['ssm_a_re'] = -0.5 + nrm(ks[10], (DEPTH, SSM_G, SSM_P), 0.01)
    inp['ssm_a_im'] = jnp.pi * jnp.arange(SSM_P, dtype=f32) + nrm(ks[11], (DEPTH, SSM_G, SSM_P), 0.01)
    inp['ssm_log_step'] = jax.random.uniform(ks[12], (DEPTH, SSM_G), f32,
                                             minval=math.log(STEP_MIN), maxval=math.log(STEP_MAX))
    inp['ssm_b_re'] = nrm(ks[13], (DEPTH, SSM_G, SSM_P, SSM_GROUP), (2.0 * SSM_GROUP) ** -0.5)
    inp['ssm_b_im'] = nrm(ks[14], (DEPTH, SSM_G, SSM_P, SSM_GROUP), (2.0 * SSM_GROUP) ** -0.5)
    inp['ssm_c_re'] = nrm(ks[15], (DEPTH, SSM_G, SSM_GROUP, SSM_P), 0.5)
    inp['ssm_c_im'] = nrm(ks[16], (DEPTH, SSM_G, SSM_GROUP, SSM_P), 0.5)
    inp['ssm_d'] = nrm(ks[17], (DEPTH, SSM_W), 1.0)
    inp['w_glu'] = nrm(ks[18], (DEPTH, SSM_W, SSM_W), SSM_W ** -0.5)
    inp['b_glu'] = nrm(ks[19], (DEPTH, SSM_W), 0.01)
    inp['w_branch_a'] = nrm(ks[20], (DEPTH, HG_V, D_MODEL), HG_V ** -0.5)
    inp['w_branch_b'] = nrm(ks[21], (DEPTH, SSM_W, D_MODEL), SSM_W ** -0.5)
    inp['w_out'] = nrm(ks[22], (DEPTH, D_MODEL, D_MODEL), D_MODEL ** -0.5)
    inp['norm2_w'] = 1.0 + nrm(ks[23], (DEPTH, D_MODEL), 0.01)
    inp['peer_wq'] = nrm(ks[24], (DEPTH, D_MODEL, PEER_HEADS * PEER_DK), D_MODEL ** -0.5)
    inp['peer_k1'] = nrm(ks[25], (DEPTH, PEER_HEADS, PEER_NKEYS, PEER_DK // 2), (PEER_DK // 2) ** -0.5)
    inp['peer_k2'] = nrm(ks[26], (DEPTH, PEER_HEADS, PEER_NKEYS, PEER_DK // 2), (PEER_DK // 2) ** -0.5)
    inp['peer_u'] = nrm(ks[27], (DEPTH, PEER_EXPERTS, D_MODEL), D_MODEL ** -0.5)
    inp['peer_v'] = nrm(ks[28], (DEPTH, PEER_EXPERTS, D_MODEL), 0.5 * PEER_HEADS ** -0.5)
    inp['final_norm_w'] = 1.0 + nrm(ks[29], (D_MODEL,), 0.01)
    return inp


def reference(x_prompt, x_sample, state_hgrn, state_ssm_re, state_ssm_im, meta_tokens, lower_bounds,
              norm1_w, w_in, g_norm_w, ssm_a_re, ssm_a_im, ssm_log_step, ssm_b_re, ssm_b_im,
              ssm_c_re, ssm_c_im, ssm_d, w_glu, b_glu, w_branch_a, w_branch_b, w_out, norm2_w,
              peer_wq, peer_k1, peer_k2, peer_u, peer_v, final_norm_w):
    lb_all = jnp.cumsum(jax.nn.softmax(lower_bounds.astype(jnp.float32), axis=0), axis=0)
    nb = x_prompt.shape[0]
    meta = jnp.broadcast_to(meta_tokens[None].astype(x_prompt.dtype), (nb, N_META, D_MODEL))
    hp = jnp.concatenate([meta, x_prompt], axis=1)
    hs = x_sample
    hg0 = jnp.zeros((nb, HG_HEADS, HG_KDIM, HG_VDIM), state_hgrn.dtype)
    re0 = jnp.zeros((nb, SSM_G, SSM_P), state_ssm_re.dtype)
    im0 = jnp.zeros((nb, SSM_G, SSM_P), state_ssm_im.dtype)
    p_hg, p_re, p_im, s_hg, s_re, s_im = [], [], [], [], [], []
    for li in range(DEPTH):
        params = (norm1_w[li], w_in[li], g_norm_w[li], ssm_a_re[li], ssm_a_im[li], ssm_log_step[li],
                  ssm_b_re[li], ssm_b_im[li], ssm_c_re[li], ssm_c_im[li], ssm_d[li], w_glu[li], b_glu[li],
                  w_branch_a[li], w_branch_b[li], w_out[li], norm2_w[li], peer_wq[li], peer_k1[li],
                  peer_k2[li], peer_u[li], peer_v[li])
        hp, a1, a2, a3 = layer(hp, hg0, re0, im0, lb_all[li], *params)
        hs, b1, b2, b3 = layer(hs, state_hgrn[li], state_ssm_re[li], state_ssm_im[li], lb_all[li], *params)
        p_hg.append(a1); p_re.append(a2); p_im.append(a3)
        s_hg.append(b1); s_re.append(b2); s_im.append(b3)
    y_prompt = rmsnorm(hp[:, N_META:], final_norm_w)
    y_sample = rmsnorm(hs, final_norm_w)
    return (y_prompt, y_sample, jnp.stack(p_hg), jnp.stack(p_re), jnp.stack(p_im),
            jnp.stack(s_hg), jnp.stack(s_re), jnp.stack(s_im))
```

```python
import functools
import math

import numpy as np
import jax
import jax.numpy as jnp
from jax import lax
from jax.experimental import pallas as pl
from jax.experimental.pallas import tpu as pltpu

F32 = jnp.float32
BF16 = jnp.bfloat16

D_MODEL = 1024
N_META = 16
HG_HEADS = 8
HG_DIM = 128
SSM_G = 64
SSM_GROUP = 16
SSM_P = 64
SSM_MIN_RE = -1e-4
S5_CHUNK = 16
PEER_HEADS = 8
PEER_NKEYS = 128
PEER_TOPK = 16
PEER_HALF = 128
EPS = 1e-6
LANES = 128
VMEM_LIMIT = 56 * 1024 * 1024


def _dot(a, b):
    return jnp.dot(a, b, preferred_element_type=F32)


def _dot_nt(a, b):
    return lax.dot_general(a, b, (((1,), (1,)), ((), ())), preferred_element_type=F32)


def _dot_tn(a, b):
    return lax.dot_general(a, b, (((0,), (0,)), ((), ())), preferred_element_type=F32)


def _split_bf16(x):
    hi = x.astype(BF16)
    lo = (x - hi.astype(F32)).astype(BF16)
    return hi, lo


def _sigmoid(x):
    return 1.0 / (1.0 + jnp.exp(-x))


def _gelu_exact(x):
    return 0.5 * x * (1.0 + lax.erf(x * (1.0 / math.sqrt(2.0))))


N_SEG = 7


def _inproj_body(x_ref, nw_ref, lbs_ref, w_ref, o_ref, lg_ref, h_scr):
    j = pl.program_id(1)

    @pl.when(j == 0)
    def _():
        x = x_ref[...]
        ms = jnp.mean(x * x, axis=-1, keepdims=True)
        h_scr[...] = (x * lax.rsqrt(ms + EPS) * nw_ref[...]).astype(BF16)

    p = _dot(h_scr[...], w_ref[...])

    @pl.when((j == 0) | (j == 3))
    def _():
        o_ref[...] = p * _sigmoid(p)

    @pl.when(j == 1)
    def _():
        lbs = lbs_ref[...]
        e = jnp.exp(lbs - jnp.max(lbs, axis=0, keepdims=True))
        lb = e[0:1] / jnp.sum(e, axis=0, keepdims=True)
        fg = lb + (1.0 - lb) * _sigmoid(p)
        o_ref[...] = 1.0 - fg
        lg_ref[...] = jnp.log(fg)

    @pl.when((j == 2) | (j == 4))
    def _():
        o_ref[...] = p

    @pl.when(j >= 5)
    def _():
        o_ref[...] = _sigmoid(p)


def _inproj(x, norm_w, lower_bounds, w_in_bf16, tm):
    t = x.shape[0]
    assert t % tm == 0
    return pl.pallas_call(
        _inproj_body,
        grid=(t // tm, N_SEG),
        in_specs=[
            pl.BlockSpec((tm, D_MODEL), lambda i, j: (i, 0)),
            pl.BlockSpec((1, D_MODEL), lambda i, j: (0, 0)),
            pl.BlockSpec(lower_bounds.shape, lambda i, j: (0, 0)),
            pl.BlockSpec((D_MODEL, D_MODEL), lambda i, j: (0, j)),
        ],
        out_specs=[
            pl.BlockSpec((None, tm, D_MODEL), lambda i, j: (j, i, 0)),
            pl.BlockSpec((tm, D_MODEL), lambda i, j: (i, 0)),
        ],
        out_shape=[
            jax.ShapeDtypeStruct((N_SEG, t, D_MODEL), F32),
            jax.ShapeDtypeStruct((t, D_MODEL), F32),
        ],
        scratch_shapes=[pltpu.VMEM((tm, D_MODEL), BF16)],
        compiler_params=pltpu.CompilerParams(
            dimension_semantics=("arbitrary", "arbitrary"), vmem_limit_bytes=VMEM_LIMIT),
        name="inproj",
    )(x, norm_w.reshape(1, D_MODEL), lower_bounds, w_in_bf16)


def _hgrn_consts(c):
    nlev = int(round(math.log2(c)))
    assert 1 << nlev == c
    d = np.zeros(((nlev + 2) * c, c), np.float32)
    for t in range(c):
        d[t, :t + 1] = 1.0
        d[c + t, t + 1:] = 1.0
        for l in range(1, nlev + 1):
            blk = 1 << l
            half = blk >> 1
            pos = t % blk
            m = t - pos + half
            row = (1 + l) * c + t
            if pos >= half:
                d[row, m:t + 1] = 1.0
            else:
                d[row, t + 1:m] = 1.0
    lev = np.full((c, c), -1, np.int32)
    for t in range(c):
        for s in range(t + 1):
            lev[t, s] = (t ^ s).bit_length()
    return d, lev, nlev


def _hgrn_body(q_ref, k_ref, v_ref, lg_ref, s0_ref, d_ref, lev_ref, o_ref, sout_ref, s_scr,
               *, c, nlev):
    r = pl.program_id(1)

    @pl.when(r == 0)
    def _():
        s_scr[...] = s0_ref[...]

    dmat = d_ref[...]
    lev = lev_ref[...]
    ones = jnp.ones((c, HG_DIM), BF16)
    tidx = lax.broadcasted_iota(jnp.int32, (c, HG_DIM), 0)
    for h in range(HG_HEADS):
        hs = slice(h * HG_DIM, (h + 1) * HG_DIM)
        q = q_ref[:, hs]
        k = k_ref[:, hs]
        vb = v_ref[:, hs].astype(BF16)
        lg_hi, lg_lo = _split_bf16(lg_ref[:, hs])
        x = jnp.exp(_dot(dmat, lg_hi) + _dot(dmat, lg_lo))
        s_old = s_scr[h]
        qd = (q * x[0:c]).astype(BF16)
        kd = (k * x[c:2 * c]).astype(BF16)
        o = _dot(qd, s_old.astype(BF16))
        att = jnp.where(lev == 0, _dot_nt(q.astype(BF16), k.astype(BF16)), 0.0)
        for l in range(1, nlev + 1):
            upper = ((tidx >> (l - 1)) & 1) == 1
            m = (jnp.where(upper, q, k) * x[(1 + l) * c:(2 + l) * c]).astype(BF16)
            att = jnp.where(lev == l, _dot_nt(m, m), att)
        o_ref[:, hs] = o + _dot(att.astype(BF16), vb)
        gl = _dot_tn(lg_hi, ones) + _dot_tn(lg_lo, ones)
        s_scr[h] = jnp.exp(gl) * s_old + _dot_tn(kd, vb)

    @pl.when(r == pl.num_programs(1) - 1)
    def _():
        sout_ref[...] = s_scr[...]


def _hgrn(q, k, v, lg, s0, nseq, nchunks, c, row0=0, slots=None):
    dnp, levnp, nlev = _hgrn_consts(c)
    assert row0 % c == 0
    blk0 = row0 // c
    if slots is None:
        tok_specs = [pl.BlockSpec((c, D_MODEL), lambda n, r: (blk0 + n * nchunks + r, 0))] * 3
    else:
        tok_specs = [pl.BlockSpec((None, c, D_MODEL),
                                  functools.partial(lambda n, r, s: (s, blk0 + n * nchunks + r, 0), s=s))
                     for s in slots]
    lg_spec = pl.BlockSpec((c, D_MODEL), lambda n, r: (blk0 + n * nchunks + r, 0))
    if s0.shape[0] == 1:
        s0_spec = pl.BlockSpec((None, HG_HEADS, HG_DIM, HG_DIM), lambda n, r: (0, 0, 0, 0))
    else:
        s0_spec = pl.BlockSpec((None, HG_HEADS, HG_DIM, HG_DIM), lambda n, r: (n, 0, 0, 0))
    return pl.pallas_call(
        functools.partial(_hgrn_body, c=c, nlev=nlev),
        grid=(nseq, nchunks),
        in_specs=tok_specs + [
            lg_spec, s0_spec,
            pl.BlockSpec(dnp.shape, lambda n, r: (0, 0)),
            pl.BlockSpec(levnp.shape, lambda n, r: (0, 0)),
        ],
        out_specs=[
            pl.BlockSpec((c, D_MODEL), lambda n, r: (n * nchunks + r, 0)),
            pl.BlockSpec((None, HG_HEADS, HG_DIM, HG_DIM), lambda n, r: (n, 0, 0, 0)),
        ],
        out_shape=[
            jax.ShapeDtypeStruct((nseq * nchunks * c, D_MODEL), F32),
            jax.ShapeDtypeStruct((nseq, HG_HEADS, HG_DIM, HG_DIM), F32),
        ],
        scratch_shapes=[pltpu.VMEM((HG_HEADS, HG_DIM, HG_DIM), F32)],
        compiler_params=pltpu.CompilerParams(
            dimension_semantics=("arbitrary", "arbitrary"), vmem_limit_bytes=VMEM_LIMIT),
        name=f"hgrn_c{c}",
    )(q, k, v, lg, s0, jnp.asarray(dnp, BF16), jnp.asarray(levnp))


def _s5_operators(a_re, a_im, log_step, b_re, b_im, c_re, c_im, d_skip, ntok):
    hp = lax.Precision.HIGHEST
    kc = S5_CHUNK
    lam = lax.complex(jnp.minimum(a_re.astype(F32), SSM_MIN_RE), a_im.astype(F32))
    delta = jnp.exp(log_step.astype(F32))[:, None]
    a_bar = jnp.exp(lam * delta)
    b_bar = ((a_bar - 1.0) / lam)[..., None] * lax.complex(b_re.astype(F32), b_im.astype(F32))
    cc = lax.complex(c_re.astype(F32), c_im.astype(F32))
    n = jnp.arange(kc + 1, dtype=F32)
    pw = jnp.exp((lam * delta)[None] * n[:, None, None])
    kl = jnp.real(jnp.einsum('gcp,ngp,gpd->ngdc', cc, pw[:kc], b_bar, precision=hp))
    s_i = np.arange(kc)[:, None]
    t_i = np.arange(kc)[None, :]
    valid = (t_i >= s_i) & (t_i < ntok)
    tz = kl[np.clip(t_i - s_i, 0, kc - 1)] * jnp.asarray(valid, F32)[:, :, None, None, None]
    eye_t = jnp.asarray((s_i == t_i) & (t_i < ntok), F32)
    dsk = d_skip.astype(F32).reshape(SSM_G, SSM_GROUP)
    tz = tz + eye_t[:, :, None, None, None] * (dsk[:, :, None] * jnp.eye(SSM_GROUP, dtype=F32)[None])[None, None]
    tz = jnp.transpose(tz, (2, 0, 3, 1, 4)).reshape(SSM_G // 2, 2, kc * SSM_GROUP, kc * SSM_GROUP)
    s_idx = np.arange(kc)
    e_in = np.clip(ntok - 1 - s_idx, 0, kc)
    m_in = jnp.asarray(s_idx < ntok, F32)
    wst = pw[e_in][:, :, :, None] * b_bar[None] * m_in[:, None, None, None]
    wst = jnp.transpose(wst, (1, 0, 3, 2)).reshape(SSM_G, kc * SSM_GROUP, SSM_P)
    e_out = np.clip(s_idx + 1, 0, kc)
    wo = cc[None] * pw[e_out][:, :, None, :] * m_in[:, None, None, None]
    wo = jnp.transpose(wo, (1, 3, 0, 2)).reshape(SSM_G, SSM_P, kc * SSM_GROUP)

    def pair_rows(w):
        w = w.reshape(SSM_G // 2, 2, kc * SSM_GROUP, SSM_P)
        z = jnp.zeros_like(w[:, 0])
        top = jnp.concatenate([w[:, 0], z], axis=-1)
        bot = jnp.concatenate([z, w[:, 1]], axis=-1)
        return jnp.concatenate([top, bot], axis=1)

    def pair_cols(w):
        w = w.reshape(SSM_G // 2, 2, SSM_P, kc * SSM_GROUP)
        z = jnp.zeros_like(w[:, 0])
        top = jnp.concatenate([w[:, 0], z], axis=-1)
        bot = jnp.concatenate([z, w[:, 1]], axis=-1)
        return jnp.concatenate([top, bot], axis=1)

    apow = pw[ntok].reshape(SSM_G // 2, 1, 2 * SSM_P)
    apow = jnp.concatenate([jnp.real(apow), jnp.imag(apow)], axis=1)
    return dict(
        tz=tz.astype(BF16),
        wst_re=pair_rows(jnp.real(wst)), wst_im=pair_rows(jnp.imag(wst)),
        wo_re=pair_cols(jnp.real(wo)).astype(BF16), wo_im=pair_cols(-jnp.imag(wo)).astype(BF16),
        apow=apow)


def _s5_body(u_ref, tz_ref, wre_ref, wim_ref, wore_ref, woim_ref, a_ref, x0re_ref, x0im_ref,
             y_ref, xre_ref, xim_ref, zre_scr, zim_scr, *, nchunks, nseq):
    half = S5_CHUNK * SSM_GROUP
    u = u_ref[...]
    uh, ul = _split_bf16(u)

    def to_state(w_ref):
        wh, wl = _split_bf16(w_ref[...])
        return _dot(uh, wh) + _dot(ul, wh) + _dot(uh, wl)

    zre_scr[...] = to_state(wre_ref)
    zim_scr[...] = to_state(wim_ref)
    a_re = a_ref[0:1, :]
    a_im = a_ref[1:2, :]

    def step(r, carry):
        x_re, x_im = carry
        rows = pl.ds(pl.multiple_of(r * nseq, 8), nseq)
        z_re = zre_scr[rows, :]
        z_im = zim_scr[rows, :]
        zre_scr[rows, :] = x_re
        zim_scr[rows, :] = x_im
        return (a_re * x_re - a_im * x_im + z_re, a_re * x_im + a_im * x_re + z_im)

    x_re, x_im = lax.fori_loop(0, nchunks, step, (x0re_ref[...], x0im_ref[...]))
    xre_ref[...] = x_re
    xim_ref[...] = x_im
    ys = _dot(zre_scr[...].astype(BF16), wore_ref[...]) + _dot(zim_scr[...].astype(BF16), woim_ref[...])
    y_ref[:, :half] = ys[:, :half] + _dot(uh[:, :half], tz_ref[0])
    y_ref[:, half:] = ys[:, half:] + _dot(uh[:, half:], tz_ref[1])


def _s5(u2, ops, x0_re, x0_im, nchunks, nseq):
    npair = SSM_G // 2
    rows = nchunks * nseq
    wide = 2 * S5_CHUNK * SSM_GROUP
    lanes = 2 * SSM_P
    per_pair = lambda shape: pl.BlockSpec((None,) + shape, lambda g: (g,) + (0,) * len(shape))
    return pl.pallas_call(
        functools.partial(_s5_body, nchunks=nchunks, nseq=nseq),
        grid=(npair,),
        in_specs=[
            per_pair((rows, wide)),
            per_pair((2, wide // 2, wide // 2)),
            per_pair((wide, lanes)), per_pair((wide, lanes)),
            per_pair((lanes, wide)), per_pair((lanes, wide)),
            per_pair((2, lanes)),
            per_pair((nseq, lanes)), per_pair((nseq, lanes)),
        ],
        out_specs=[per_pair((rows, wide)), per_pair((nseq, lanes)), per_pair((nseq, lanes))],
        out_shape=[
            jax.ShapeDtypeStruct((npair, rows, wide), F32),
            jax.ShapeDtypeStruct((npair, nseq, lanes), F32),
            jax.ShapeDtypeStruct((npair, nseq, lanes), F32),
        ],
        scratch_shapes=[pltpu.VMEM((rows, lanes), F32), pltpu.VMEM((rows, lanes), F32)],
        compiler_params=pltpu.CompilerParams(
            dimension_semantics=("arbitrary",), vmem_limit_bytes=VMEM_LIMIT),
        name=f"s5_r{rows}",
    )(u2, ops['tz'], ops['wst_re'], ops['wst_im'], ops['wo_re'], ops['wo_im'], ops['apow'], x0_re, x0_im)


def _s5_pack(u, nseq, ntok):
    nchunks = max(ntok // S5_CHUNK, 1)
    per = ntok // nchunks
    u = u.reshape(nseq, nchunks, per, SSM_G // 2, 2, SSM_GROUP)
    if per < S5_CHUNK:
        u = jnp.pad(u, ((0, 0), (0, 0), (0, S5_CHUNK - per), (0, 0), (0, 0), (0, 0)))
    u = jnp.transpose(u, (3, 1, 0, 4, 2, 5))
    return u.reshape(SSM_G // 2, nchunks * nseq, 2 * S5_CHUNK * SSM_GROUP)


def _s5_unpack(y2, nseq, ntok):
    nchunks = max(ntok // S5_CHUNK, 1)
    per = ntok // nchunks
    y = y2.reshape(SSM_G // 2, nchunks, nseq, 2, S5_CHUNK, SSM_GROUP)[:, :, :, :, :per]
    y = jnp.transpose(y, (2, 1, 4, 0, 3, 5))
    return y.reshape(nseq * ntok, SSM_G * SSM_GROUP)


def _state_to_pairs(x):
    return jnp.transpose(x.reshape(x.shape[0], SSM_G // 2, 2 * SSM_P), (1, 0, 2))


def _pairs_to_state(x):
    return jnp.transpose(x, (1, 0, 2)).reshape(x.shape[1], SSM_G, SSM_P)


def _post_body(x_ref, o_ref, sg_ref, y_ref, sa_ref, sb_ref, gnw_ref, wglu_ref, bglu_ref,
               wa_ref, wb_ref, wout_ref, out_ref, a_scr):
    gnw = gnw_ref[...]
    for h in range(HG_HEADS):
        hs = slice(h * HG_DIM, (h + 1) * HG_DIM)
        oh = o_ref[:, hs]
        ms = jnp.mean(oh * oh, axis=-1, keepdims=True)
        a_scr[:, hs] = (oh * lax.rsqrt(ms + EPS) * gnw * sg_ref[:, hs]).astype(BF16)
    br_a = _dot(a_scr[...], wa_ref[...])
    y = _gelu_exact(y_ref[...])
    y = y * _sigmoid(_dot(y.astype(BF16), wglu_ref[...]) + bglu_ref[...])
    br_b = _dot(y.astype(BF16), wb_ref[...])
    mixed = sa_ref[...] * br_a + sb_ref[...] * br_b
    out_ref[...] = x_ref[...] + _dot(mixed.astype(BF16), wout_ref[...])


def _post(x, o, y, proj, g_norm_w, w_glu, b_glu, w_a, w_b, w_out, tm):
    t = x.shape[0]
    assert t % tm == 0
    tok = pl.BlockSpec((tm, D_MODEL), lambda i: (i, 0))
    slot = lambda s: pl.BlockSpec((None, tm, D_MODEL), lambda i: (s, i, 0))
    full = lambda shape: pl.BlockSpec(shape, lambda i: (0,) * len(shape))
    sq = (D_MODEL, D_MODEL)
    return pl.pallas_call(
        _post_body,
        grid=(t // tm,),
        in_specs=[tok, tok, slot(3), tok, slot(5), slot(6), full((1, HG_DIM)), full(sq),
                  full((1, D_MODEL)), full(sq), full(sq), full(sq)],
        out_specs=tok,
        out_shape=jax.ShapeDtypeStruct((t, D_MODEL), F32),
        scratch_shapes=[pltpu.VMEM((tm, D_MODEL), BF16)],
        compiler_params=pltpu.CompilerParams(
            dimension_semantics=("arbitrary",), vmem_limit_bytes=VMEM_LIMIT),
        name="post",
    )(x, o, proj, y, proj, proj, g_norm_w.reshape(1, HG_DIM), w_glu.astype(BF16),
      b_glu.reshape(1, D_MODEL), w_a.astype(BF16), w_b.astype(BF16), w_out.astype(BF16))


def _top16_rows(w):
    rows = []
    for _ in range(PEER_TOPK):
        m = jnp.max(w, axis=0, keepdims=True)
        rows.append(m)
        w = jnp.where(w == m, -jnp.inf, w)
    return rows


def _peer_body(x_ref, n2_ref, fn_ref, wq_ref, k1_ref, k2_ref, u_ref, vt_ref, y_ref,
               hn_scr, s_scr, p_scr, tau_scr, vs_scr, act_scr, g_scr, acc_scr, *, tm, eb):
    j = pl.program_id(1)
    nlt = tm // LANES
    nrow = eb // PEER_NKEYS
    assert nrow == 8

    @pl.when(j == 0)
    def _():
        x = x_ref[...]
        ms = jnp.mean(x * x, axis=-1, keepdims=True)
        hn_scr[...] = (x * lax.rsqrt(ms + EPS) * n2_ref[...]).astype(BF16)
        hn = hn_scr[...]
        for h in range(PEER_HEADS):
            for half, k_ref in enumerate((k1_ref, k2_ref)):
                c0 = (2 * h + half) * PEER_HALF
                qh = _dot(hn, wq_ref[:, c0:c0 + PEER_HALF]).astype(BF16)
                s_scr[2 * h + half] = _dot_nt(k_ref[h], qh)

        def top_body(idx, carry):
            hh = idx // nlt
            cols = pl.ds(pl.multiple_of((idx % nlt) * LANES, LANES), LANES)
            rows = _top16_rows(s_scr[hh, :, cols])
            vs_scr[hh, :, cols] = jnp.concatenate(rows, axis=0)
            return carry

        lax.fori_loop(0, 2 * PEER_HEADS * nlt, top_body, 0)

        def gate_body(idx, carry):
            h = idx // nlt
            cols = pl.ds(pl.multiple_of((idx % nlt) * LANES, LANES), LANES)
            v1 = vs_scr[2 * h, :, cols]
            v2 = vs_scr[2 * h + 1, :, cols]
            cand = jnp.concatenate([v1[i:i + 1] + v2 for i in range(PEER_TOPK)], axis=0)
            tau = _top16_rows(cand)[-1]
            top = v1[0:1] + v2[0:1]
            z = jnp.sum(jnp.where(cand >= tau, jnp.exp(cand - top), 0.0), axis=0, keepdims=True)
            tau_scr[h, :, cols] = jnp.broadcast_to(tau, (8, LANES))
            s1 = s_scr[2 * h, :, cols]
            s2 = s_scr[2 * h + 1, :, cols]
            p_scr[2 * h, :, cols] = jnp.where(s1 >= v1[PEER_TOPK - 1:], jnp.exp(s1 - v1[0:1]), 0.0) / z
            p_scr[2 * h + 1, :, cols] = jnp.where(s2 >= v2[PEER_TOPK - 1:], jnp.exp(s2 - v2[0:1]), 0.0)
            return carry

        lax.fori_loop(0, PEER_HEADS * nlt, gate_body, 0)

    act_scr[...] = _dot_nt(u_ref[...], hn_scr[...])

    e1_rows = pl.ds(pl.multiple_of(j * nrow, nrow), nrow)

    def w_body(lt, carry):
        cols = pl.ds(pl.multiple_of(lt * LANES, LANES), LANES)
        for r in range(nrow):
            w = jnp.zeros((PEER_NKEYS, LANES), F32)
            for h in range(PEER_HEADS):
                s1row = s_scr[2 * h, e1_rows, cols][r:r + 1]
                p1row = p_scr[2 * h, e1_rows, cols][r:r + 1]
                s2 = s_scr[2 * h + 1, :, cols]
                p2 = p_scr[2 * h + 1, :, cols]
                tau = tau_scr[h, 0:1, cols]
                w = w + jnp.where(s1row + s2 >= tau, p2 * p1row, 0.0)
            rows = slice(r * PEER_NKEYS, (r + 1) * PEER_NKEYS)
            g_scr[rows, cols] = (w * _gelu_exact(act_scr[rows, cols])).astype(BF16)
        return carry

    lax.fori_loop(0, nlt, w_body, 0)
    contrib = _dot(vt_ref[...], g_scr[...])

    @pl.when(j == 0)
    def _():
        acc_scr[...] = contrib

    @pl.when(j > 0)
    def _():
        acc_scr[...] += contrib

    @pl.when(j == pl.num_programs(1) - 1)
    def _():
        xo = x_ref[...] + acc_scr[...].T
        ms = jnp.mean(xo * xo, axis=-1, keepdims=True)
        y_ref[...] = xo * lax.rsqrt(ms + EPS) * fn_ref[...]


def _peer(x, norm2_w, final_norm_w, wq, k1, k2, u_tab, v_tab, tm, eb):
    t = x.shape[0]
    assert t % tm == 0
    nexp = u_tab.shape[0]
    assert nexp == PEER_NKEYS * PEER_NKEYS and nexp % eb == 0 and eb % PEER_NKEYS == 0
    qw = 2 * PEER_HEADS * PEER_HALF
    full = lambda shape: pl.BlockSpec(shape, lambda i, j: (0,) * len(shape))
    return pl.pallas_call(
        functools.partial(_peer_body, tm=tm, eb=eb),
        grid=(t // tm, nexp // eb),
        in_specs=[
            pl.BlockSpec((tm, D_MODEL), lambda i, j: (i, 0)),
            full((1, D_MODEL)), full((1, D_MODEL)),
            full((D_MODEL, qw)),
            full((PEER_HEADS, PEER_NKEYS, PEER_HALF)), full((PEER_HEADS, PEER_NKEYS, PEER_HALF)),
            pl.BlockSpec((eb, D_MODEL), lambda i, j: (j, 0)),
            pl.BlockSpec((D_MODEL, eb), lambda i, j: (0, j)),
        ],
        out_specs=pl.BlockSpec((tm, D_MODEL), lambda i, j: (i, 0)),
        out_shape=jax.ShapeDtypeStruct((t, D_MODEL), F32),
        scratch_shapes=[
            pltpu.VMEM((tm, D_MODEL), BF16),
            pltpu.VMEM((2 * PEER_HEADS, PEER_NKEYS, tm), F32),
            pltpu.VMEM((2 * PEER_HEADS, PEER_NKEYS, tm), F32),
            pltpu.VMEM((PEER_HEADS, 8, tm), F32),
            pltpu.VMEM((2 * PEER_HEADS, PEER_TOPK, tm), F32),
            pltpu.VMEM((eb, tm), F32),
            pltpu.VMEM((eb, tm), BF16),
            pltpu.VMEM((D_MODEL, tm), F32),
        ],
        compiler_params=pltpu.CompilerParams(
            dimension_semantics=("arbitrary", "arbitrary"), vmem_limit_bytes=VMEM_LIMIT),
        name="peer",
    )(x, norm2_w.reshape(1, D_MODEL), final_norm_w.reshape(1, D_MODEL), wq.astype(BF16),
      k1.astype(BF16), k2.astype(BF16), u_tab.astype(BF16), v_tab.astype(BF16).T)


def _pad_seq(a, nseq, ntok, c):
    a = a.reshape(nseq, ntok, D_MODEL)
    return jnp.pad(a, ((0, 0), (0, c - ntok), (0, 0))).reshape(nseq * c, D_MODEL)


def kernel(x_prompt, x_sample, state_hgrn, state_ssm_re, state_ssm_im, meta_tokens, lower_bounds,
           norm1_w, w_in, g_norm_w, ssm_a_re, ssm_a_im, ssm_log_step, ssm_b_re, ssm_b_im,
           ssm_c_re, ssm_c_im, ssm_d, w_glu, b_glu, w_branch_a, w_branch_b, w_out, norm2_w,
           peer_wq, peer_k1, peer_k2, peer_u, peer_v, final_norm_w):
    depth = w_in.shape[0]
    assert depth == 1, "single-layer step only"
    nb, seq, _ = x_prompt.shape
    ns, dseq, _ = x_sample.shape
    tp = nb * seq
    tsm = ns * dseq
    hg_c = 128
    small_c = 16
    assert seq % hg_c == 0 and seq % S5_CHUNK == 0 and dseq <= small_c and ns % 8 == 0

    w_in_b = w_in[0].astype(BF16)
    x_all = jnp.concatenate([x_prompt.reshape(tp, D_MODEL), x_sample.reshape(tsm, D_MODEL)], axis=0)
    tm = 512
    assert (tp + tsm) % tm == 0
    proj, lg = _inproj(x_all, norm1_w[0], lower_bounds, w_in_b, tm)
    proj_m, lg_m = _inproj(meta_tokens.astype(F32), norm1_w[0], lower_bounds, w_in_b, N_META)

    zero_hg = jnp.zeros((1, HG_HEADS, HG_DIM, HG_DIM), F32)
    _, hg_meta = _hgrn(proj_m[0], proj_m[1], proj_m[2], lg_m, zero_hg, 1, 1, N_META)
    o_p, hg_p = _hgrn(proj, proj, proj, lg, hg_meta, nb, seq // hg_c, hg_c, slots=(0, 1, 2))
    pad = lambda a: _pad_seq(a[tp:], ns, dseq, small_c)
    o_s, hg_s = _hgrn(pad(proj[0]), pad(proj[1]), pad(proj[2]), pad(lg), state_hgrn[0].astype(F32),
                      ns, 1, small_c)
    o_s = o_s.reshape(ns, small_c, D_MODEL)[:, :dseq].reshape(tsm, D_MODEL)
    o_all = jnp.concatenate([o_p, o_s], axis=0)

    ssm = (ssm_a_re[0], ssm_a_im[0], ssm_log_step[0], ssm_b_re[0], ssm_b_im[0], ssm_c_re[0],
           ssm_c_im[0], ssm_d[0])
    ops16 = _s5_operators(*ssm, S5_CHUNK)
    ops_s = _s5_operators(*ssm, dseq)
    u_meta = jnp.pad(proj_m[4], ((0, 7 * N_META), (0, 0)))
    zero_ss = jnp.zeros((SSM_G // 2, 8, 2 * SSM_P), F32)
    _, mre, mim = _s5(_s5_pack(u_meta, 8, N_META), ops16, zero_ss, zero_ss, 1, 8)
    x0re = jnp.broadcast_to(mre[:, 0:1], (SSM_G // 2, nb, 2 * SSM_P))
    x0im = jnp.broadcast_to(mim[:, 0:1], (SSM_G // 2, nb, 2 * SSM_P))
    y2_p, pre, pim = _s5(_s5_pack(proj[4, :tp], nb, seq), ops16, x0re, x0im, seq // S5_CHUNK, nb)
    y2_s, sre, sim = _s5(_s5_pack(proj[4, tp:], ns, dseq), ops_s,
                         _state_to_pairs(state_ssm_re[0].astype(F32)),
                         _state_to_pairs(state_ssm_im[0].astype(F32)), 1, ns)
    y_all = jnp.concatenate([_s5_unpack(y2_p, nb, seq), _s5_unpack(y2_s, ns, dseq)], axis=0)

    x1 = _post(x_all, o_all, y_all, proj, g_norm_w[0], w_glu[0], b_glu[0], w_branch_a[0],
               w_branch_b[0], w_out[0], tm)
    y = _peer(x1, norm2_w[0], final_norm_w, peer_wq[0], peer_k1[0], peer_k2[0], peer_u[0],
              peer_v[0], tm, 1024)

    sd = state_hgrn.dtype
    return (y[:tp].reshape(nb, seq, D_MODEL).astype(x_prompt.dtype),
            y[tp:].reshape(ns, dseq, D_MODEL).astype(x_sample.dtype),
            hg_p[None].astype(sd),
            _pairs_to_state(pre)[None].astype(state_ssm_re.dtype),
            _pairs_to_state(pim)[None].astype(state_ssm_im.dtype),
            hg_s[None].astype(sd),
            _pairs_to_state(sre)[None].astype(state_ssm_re.dtype),
            _pairs_to_state(sim)[None].astype(state_ssm_im.dtype))
```

```python
import functools
import math

import numpy as np
import jax
import jax.numpy as jnp
from jax import lax
from jax.experimental import pallas as pl
from jax.experimental.pallas import tpu as pltpu

F32 = jnp.float32
BF16 = jnp.bfloat16

D_MODEL = 1024
N_META = 16
HG_HEADS = 8
HG_DIM = 128
SSM_G = 64
SSM_GROUP = 16
SSM_P = 64
SSM_MIN_RE = -1e-4
S5_GB = 8
PEER_HEADS = 8
PEER_NKEYS = 128
PEER_TOPK = 16
PEER_HALF = 128
EPS = 1e-6
LANES = 128
SUBLANES = 8
VMEM_LIMIT = 56 * 1024 * 1024


def _dot(a, b):
    return jnp.dot(a, b, preferred_element_type=F32)


def _dot_nt(a, b):
    return lax.dot_general(a, b, (((1,), (1,)), ((), ())), preferred_element_type=F32)


def _dot_tn(a, b):
    return lax.dot_general(a, b, (((0,), (0,)), ((), ())), preferred_element_type=F32)


def _split_bf16(x):
    hi = x.astype(BF16)
    lo = (x - hi.astype(F32)).astype(BF16)
    return hi, lo


def _sigmoid(x):
    return 1.0 / (1.0 + jnp.exp(-x))


def _gelu_exact(x):
    return 0.5 * x * (1.0 + lax.erf(x * (1.0 / math.sqrt(2.0))))


def _params(*sem):
    return pltpu.CompilerParams(dimension_semantics=sem, vmem_limit_bytes=VMEM_LIMIT)


def _two_source_specs(tm, n_first, ncols=D_MODEL):
    first = pl.BlockSpec((tm, ncols), lambda i, *_: (jnp.minimum(i, n_first - 1), 0))
    second = pl.BlockSpec((tm, ncols), lambda i, *_: (jnp.maximum(i - n_first, 0), 0))
    return first, second


N_SEG = 7


def _inproj_body(xa_ref, xb_ref, nw_ref, lbs_ref, w_ref, o_ref, lg_ref, h_scr, *, n_first):
    i = pl.program_id(0)
    j = pl.program_id(1)

    def normalise(x_ref):
        x = x_ref[...]
        ms = jnp.mean(x * x, axis=-1, keepdims=True)
        h_scr[...] = (x * lax.rsqrt(ms + EPS) * nw_ref[...]).astype(BF16)

    @pl.when((j == 0) & (i < n_first))
    def _():
        normalise(xa_ref)

    @pl.when((j == 0) & (i >= n_first))
    def _():
        normalise(xb_ref)

    p = _dot(h_scr[...], w_ref[...])

    @pl.when((j == 0) | (j == 3))
    def _():
        o_ref[...] = p * _sigmoid(p)

    @pl.when(j == 1)
    def _():
        lbs = lbs_ref[...]
        e = jnp.exp(lbs - jnp.max(lbs, axis=0, keepdims=True))
        lb = e[0:1] / jnp.sum(e, axis=0, keepdims=True)
        fg = lb + (1.0 - lb) * _sigmoid(p)
        o_ref[...] = 1.0 - fg
        lg_ref[...] = jnp.log(fg)

    @pl.when((j == 2) | (j == 4))
    def _():
        o_ref[...] = p

    @pl.when(j >= 5)
    def _():
        o_ref[...] = _sigmoid(p)


def _inproj(xa, xb, norm_w, lower_bounds, w_in_bf16, tm):
    if xb is None:
        xb = xa
        t = xa.shape[0]
        n_first = t // tm
    else:
        t = xa.shape[0] + xb.shape[0]
        n_first = xa.shape[0] // tm
        assert xa.shape[0] % tm == 0 and xb.shape[0] % tm == 0
    spec_a, spec_b = _two_source_specs(tm, n_first)
    return pl.pallas_call(
        functools.partial(_inproj_body, n_first=n_first),
        grid=(t // tm, N_SEG),
        in_specs=[
            spec_a, spec_b,
            pl.BlockSpec((1, D_MODEL), lambda i, j: (0, 0)),
            pl.BlockSpec(lower_bounds.shape, lambda i, j: (0, 0)),
            pl.BlockSpec((D_MODEL, D_MODEL), lambda i, j: (0, j)),
        ],
        out_specs=[
            pl.BlockSpec((None, tm, D_MODEL), lambda i, j: (j, i, 0)),
            pl.BlockSpec((tm, D_MODEL), lambda i, j: (i, 0)),
        ],
        out_shape=[
            jax.ShapeDtypeStruct((N_SEG, t, D_MODEL), F32),
            jax.ShapeDtypeStruct((t, D_MODEL), F32),
        ],
        scratch_shapes=[pltpu.VMEM((tm, D_MODEL), BF16)],
        compiler_params=_params("arbitrary", "arbitrary"),
        name="inproj",
    )(xa, xb, norm_w.reshape(1, D_MODEL), lower_bounds, w_in_bf16)


def _hgrn_consts(c):
    nlev = int(round(math.log2(c)))
    assert 1 << nlev == c
    d = np.zeros(((nlev + 2) * c, c), np.float32)
    for t in range(c):
        d[t, :t + 1] = 1.0
        d[c + t, t + 1:] = 1.0
        for l in range(1, nlev + 1):
            blk = 1 << l
            half = blk >> 1
            pos = t % blk
            m = t - pos + half
            row = (1 + l) * c + t
            if pos >= half:
                d[row, m:t + 1] = 1.0
            else:
                d[row, t + 1:m] = 1.0
    lev = np.full((c, c), -1, np.int32)
    for t in range(c):
        for s in range(t + 1):
            lev[t, s] = (t ^ s).bit_length()
    return d, lev, nlev


def _hgrn_body(q_ref, k_ref, v_ref, lg_ref, s0_ref, d_ref, lev_ref, o_ref, sout_ref, s_scr,
               *pad_scr, c, nlev, ntok):
    n = pl.program_id(0)
    r = pl.program_id(1)

    @pl.when(r == 0)
    def _():
        s_scr[...] = s0_ref[...]

    if ntok < c:
        (pad,) = pad_scr

        @pl.when((n == 0) & (r == 0))
        def _():
            pad[...] = jnp.zeros_like(pad)

        for a, ref in enumerate((q_ref, k_ref, v_ref, lg_ref)):
            pad[a, 0:ntok, :] = ref[...]
        q_ref, k_ref, v_ref, lg_ref = (pad.at[a] for a in range(4))

    dmat = d_ref[...]
    lev = lev_ref[...]
    ones = jnp.ones((c, HG_DIM), BF16)
    tidx = lax.broadcasted_iota(jnp.int32, (c, HG_DIM), 0)
    for h in range(HG_HEADS):
        hs = slice(h * HG_DIM, (h + 1) * HG_DIM)
        q = q_ref[:, hs]
        k = k_ref[:, hs]
        vb = v_ref[:, hs].astype(BF16)
        lg_hi, lg_lo = _split_bf16(lg_ref[:, hs])
        x = jnp.exp(_dot(dmat, lg_hi) + _dot(dmat, lg_lo))
        s_old = s_scr[h]
        qd = (q * x[0:c]).astype(BF16)
        kd = (k * x[c:2 * c]).astype(BF16)
        o = _dot(qd, s_old.astype(BF16))
        att = jnp.where(lev == 0, _dot_nt(q.astype(BF16), k.astype(BF16)), 0.0)
        for l in range(1, nlev + 1):
            upper = ((tidx >> (l - 1)) & 1) == 1
            m = (jnp.where(upper, q, k) * x[(1 + l) * c:(2 + l) * c]).astype(BF16)
            att = jnp.where(lev == l, _dot_nt(m, m), att)
        o = o + _dot(att.astype(BF16), vb)
        o_ref[:, hs] = o[0:ntok]
        gl = _dot_tn(lg_hi, ones) + _dot_tn(lg_lo, ones)
        s_scr[h] = jnp.exp(gl) * s_old + _dot_tn(kd, vb)

    @pl.when(r == pl.num_programs(1) - 1)
    def _():
        sout_ref[...] = s_scr[...]


def _hgrn(proj, lg, s0, nseq, nchunks, c, row0, ntok=None):
    ntok = c if ntok is None else ntok
    dnp, levnp, nlev = _hgrn_consts(c)
    assert row0 % ntok == 0 and (ntok == c or nchunks == 1)
    blk0 = row0 // ntok
    if ntok == c:
        tok_specs = [pl.BlockSpec((None, c, D_MODEL),
                                  functools.partial(lambda n, r, s: (s, blk0 + n * nchunks + r, 0), s=s))
                     for s in range(3)]
        lg_spec = pl.BlockSpec((c, D_MODEL), lambda n, r: (blk0 + n * nchunks + r, 0))
        o_spec = pl.BlockSpec((c, D_MODEL), lambda n, r: (n * nchunks + r, 0))
        o_shape = jax.ShapeDtypeStruct((nseq * nchunks * c, D_MODEL), F32)
        scratch = []
    else:
        nslot, rows, _ = proj.shape
        proj = proj.reshape(nslot, rows // ntok, ntok, D_MODEL)
        lg = lg.reshape(rows // ntok, ntok, D_MODEL)
        tok_specs = [pl.BlockSpec((None, None, ntok, D_MODEL),
                                  functools.partial(lambda n, r, s: (s, blk0 + n, 0, 0), s=s))
                     for s in range(3)]
        lg_spec = pl.BlockSpec((None, ntok, D_MODEL), lambda n, r: (blk0 + n, 0, 0))
        o_spec = pl.BlockSpec((None, ntok, D_MODEL), lambda n, r: (n, 0, 0))
        o_shape = jax.ShapeDtypeStruct((nseq, ntok, D_MODEL), F32)
        scratch = [pltpu.VMEM((4, c, D_MODEL), F32)]
    if s0.shape[0] == 1:
        s0_spec = pl.BlockSpec((None, HG_HEADS, HG_DIM, HG_DIM), lambda n, r: (0, 0, 0, 0))
    else:
        s0_spec = pl.BlockSpec((None, HG_HEADS, HG_DIM, HG_DIM), lambda n, r: (n, 0, 0, 0))
    return pl.pallas_call(
        functools.partial(_hgrn_body, c=c, nlev=nlev, ntok=ntok),
        grid=(nseq, nchunks),
        in_specs=tok_specs + [
            lg_spec, s0_spec,
            pl.BlockSpec(dnp.shape, lambda n, r: (0, 0)),
            pl.BlockSpec(levnp.shape, lambda n, r: (0, 0)),
        ],
        out_specs=[
            o_spec,
            pl.BlockSpec((None, HG_HEADS, HG_DIM, HG_DIM), lambda n, r: (n, 0, 0, 0)),
        ],
        out_shape=[o_shape, jax.ShapeDtypeStruct((nseq, HG_HEADS, HG_DIM, HG_DIM), F32)],
        scratch_shapes=[pltpu.VMEM((HG_HEADS, HG_DIM, HG_DIM), F32)] + scratch,
        compiler_params=_params("arbitrary", "arbitrary"),
        name=f"hgrn_c{c}_t{ntok}",
    )(proj, proj, proj, lg, s0, jnp.asarray(dnp, BF16), jnp.asarray(levnp))


def _s5_operators(a_re, a_im, log_step, b_re, b_im, c_re, c_im, d_skip, ntok):
    hp = lax.Precision.HIGHEST
    nb = SSM_G // S5_GB
    lam_re = jnp.minimum(a_re.astype(F32), SSM_MIN_RE)
    lam_im = a_im.astype(F32)
    delta = jnp.exp(log_step.astype(F32))[:, None]
    n = jnp.arange(ntok + 1, dtype=F32)[:, None, None]
    mag = jnp.exp(lam_re * delta * n)
    ang = lam_im * delta * n
    pw_re = mag * jnp.cos(ang)
    pw_im = mag * jnp.sin(ang)
    x = pw_re[1] - 1.0
    y = pw_im[1]
    den = lam_re * lam_re + lam_im * lam_im
    cr = (x * lam_re + y * lam_im) / den
    ci = (y * lam_re - x * lam_im) / den
    bb_re = cr[..., None] * b_re.astype(F32) - ci[..., None] * b_im.astype(F32)
    bb_im = cr[..., None] * b_im.astype(F32) + ci[..., None] * b_re.astype(F32)
    w_re = pw_re[:ntok, :, :, None] * bb_re[None] - pw_im[:ntok, :, :, None] * bb_im[None]
    w_im = pw_re[:ntok, :, :, None] * bb_im[None] + pw_im[:ntok, :, :, None] * bb_re[None]
    cc_re = c_re.astype(F32)
    cc_im = c_im.astype(F32)
    kl = (jnp.einsum('gcp,ngpd->ngdc', cc_re, w_re, precision=hp)
          - jnp.einsum('gcp,ngpd->ngdc', cc_im, w_im, precision=hp))
    kl = kl.at[0].add(d_skip.astype(F32).reshape(SSM_G, SSM_GROUP)[:, :, None]
                      * jnp.eye(SSM_GROUP, dtype=F32)[None])
    zero = jnp.zeros_like(kl[0])
    base = jnp.stack([jnp.stack([kl[t - s] if t >= s else zero for t in range(ntok)], axis=0)
                      for s in range(ntok)], axis=0)
    eye_g = jnp.eye(S5_GB, dtype=F32)
    base = base.reshape(ntok, ntok, nb, S5_GB, SSM_GROUP, SSM_GROUP)
    base = jnp.transpose(base, (2, 0, 3, 4, 1, 5))
    tz = base[:, :, :, :, :, None, :] * eye_g[None, None, :, None, None, :, None]
    tz = tz.reshape(nb, ntok, LANES, ntok * LANES)

    def state_in(w):
        w = jnp.flip(w, axis=0).reshape(ntok, nb, S5_GB, SSM_P, SSM_GROUP)
        w = jnp.transpose(w, (1, 0, 2, 4, 3))
        w = w[:, :, :, :, None, :] * eye_g[None, None, :, None, :, None]
        return w.reshape(nb, ntok, LANES, S5_GB * SSM_P)

    def state_out(m_re, m_im, sign):
        p_re = pw_re[1:ntok + 1]
        p_im = pw_im[1:ntok + 1]
        if sign > 0:
            w = m_re[None] * p_re[:, :, None, :] - m_im[None] * p_im[:, :, None, :]
        else:
            w = -(m_re[None] * p_im[:, :, None, :] + m_im[None] * p_re[:, :, None, :])
        w = w.reshape(ntok, nb, S5_GB, SSM_GROUP, SSM_P)
        w = jnp.transpose(w, (1, 2, 4, 0, 3))
        w = w[:, :, :, :, None, :] * eye_g[None, :, None, None, :, None]
        return w.reshape(nb, S5_GB * SSM_P, ntok * LANES)

    apow = jnp.stack([pw_re[ntok].reshape(nb, S5_GB * SSM_P),
                      pw_im[ntok].reshape(nb, S5_GB * SSM_P)], axis=1)
    return dict(tz=tz.astype(BF16), wst_re=state_in(w_re), wst_im=state_in(w_im),
                wo_re=state_out(cc_re, cc_im, +1).astype(BF16),
                wo_im=state_out(cc_re, cc_im, -1).astype(BF16), apow=apow)


def _s5_body(u_ref, tz_ref, wre_ref, wim_ref, wore_ref, woim_ref, a_ref, x0re_ref, x0im_ref,
             y_ref, xre_ref, xim_ref, zre_scr, zim_scr, ub_scr,
             *, ntok, nchunks, nseq, npass):
    rows = nseq * nchunks
    nlb = (S5_GB * SSM_P) // LANES
    lane_blk = lambda lb: slice(lb * LANES, (lb + 1) * LANES)
    for s in range(ntok):
        us = u_ref[pl.ds(s, rows, stride=ntok), :]
        uh, ul = _split_bf16(us)
        ub_scr[s] = uh
        for w_ref, z_scr in ((wre_ref, zre_scr), (wim_ref, zim_scr)):
            wh, wl = _split_bf16(w_ref[s])
            z = _dot(uh, wh)
            if npass >= 2:
                z = z + _dot(ul, wh)
            if npass >= 3:
                z = z + _dot(uh, wl)
            for lb in range(nlb):
                if s == 0:
                    z_scr[lb] = z[:, lane_blk(lb)]
                else:
                    z_scr[lb] += z[:, lane_blk(lb)]
    a_re = a_ref[0:1, :]
    a_im = a_ref[1:2, :]

    def step(r, carry):
        x_re, x_im = carry
        at = pl.ds(r, nseq, stride=nchunks) if nchunks > 1 else pl.ds(0, nseq)
        z_re = jnp.concatenate([zre_scr[lb, at, :] for lb in range(nlb)], axis=1)
        z_im = jnp.concatenate([zim_scr[lb, at, :] for lb in range(nlb)], axis=1)
        for lb in range(nlb):
            zre_scr[lb, at, :] = x_re[:, lane_blk(lb)]
            zim_scr[lb, at, :] = x_im[:, lane_blk(lb)]
        return (a_re * x_re - a_im * x_im + z_re, a_re * x_im + a_im * x_re + z_im)

    x_re, x_im = lax.fori_loop(0, nchunks, step, (x0re_ref[0:nseq, :], x0im_ref[0:nseq, :]))
    xre_ref[...] = jnp.zeros_like(xre_ref)
    xim_ref[...] = jnp.zeros_like(xim_ref)
    xre_ref[0:nseq, :] = x_re
    xim_ref[0:nseq, :] = x_im
    for t0 in range(0, ntok, 2):
        cols = slice(t0 * LANES, (t0 + 2) * LANES)
        acc = jnp.zeros((rows, 2 * LANES), F32)
        for lb in range(nlb):
            acc = acc + _dot(zre_scr[lb].astype(BF16), wore_ref[lane_blk(lb), cols])
            acc = acc + _dot(zim_scr[lb].astype(BF16), woim_ref[lane_blk(lb), cols])
        for s in range(t0 + 2):
            acc = acc + _dot(ub_scr[s], tz_ref[s, :, cols])
        y_ref[pl.ds(t0, rows, stride=ntok), :] = acc[:, :LANES]
        y_ref[pl.ds(t0 + 1, rows, stride=ntok), :] = acc[:, LANES:]


def _s5(u, slot, ops, x0_re, x0_im, row0, nblk, nseq, nchunks, ntok, npass):
    nb = SSM_G // S5_GB
    rows = nseq * nchunks
    trows = rows * ntok
    assert row0 % trows == 0 and ntok % 2 == 0
    blk0 = row0 // trows
    nseq_pad = x0_re.shape[1]
    lanes = S5_GB * SSM_P
    if u.ndim == 3:
        u_spec = pl.BlockSpec((None, trows, LANES), lambda g, sb: (slot, blk0 + sb, g))
    else:
        u_spec = pl.BlockSpec((trows, LANES), lambda g, sb: (blk0 + sb, g))
    per_g = lambda shape: pl.BlockSpec((None,) + shape, lambda g, sb: (g,) + (0,) * len(shape))
    st_spec = pl.BlockSpec((None, nseq_pad, lanes), lambda g, sb: (sb, 0, g))
    st_shape = jax.ShapeDtypeStruct((nblk, nseq_pad, SSM_G * SSM_P), F32)
    return pl.pallas_call(
        functools.partial(_s5_body, ntok=ntok, nchunks=nchunks, nseq=nseq, npass=npass),
        grid=(nb, nblk),
        in_specs=[
            u_spec,
            per_g((ntok, LANES, ntok * LANES)),
            per_g((ntok, LANES, lanes)), per_g((ntok, LANES, lanes)),
            per_g((lanes, ntok * LANES)), per_g((lanes, ntok * LANES)),
            per_g((2, lanes)),
            st_spec, st_spec,
        ],
        out_specs=[pl.BlockSpec((trows, LANES), lambda g, sb: (sb, g)), st_spec, st_spec],
        out_shape=[jax.ShapeDtypeStruct((nblk * trows, D_MODEL), F32), st_shape, st_shape],
        scratch_shapes=[pltpu.VMEM((lanes // LANES, rows, LANES), F32),
                        pltpu.VMEM((lanes // LANES, rows, LANES), F32),
                        pltpu.VMEM((ntok, rows, LANES), BF16)],
        compiler_params=_params("arbitrary", "arbitrary"),
        name=f"s5_r{rows}_t{ntok}",
    )(u, ops['tz'], ops['wst_re'], ops['wst_im'], ops['wo_re'], ops['wo_im'], ops['apow'],
      x0_re, x0_im)


def _post_body(xa_ref, xb_ref, oa_ref, ob_ref, ya_ref, yb_ref, sg_ref, sa_ref, sb_ref, gnw_ref,
               wglu_ref, bglu_ref, wa_ref, wb_ref, wout_ref, out_ref, a_scr, *, n_first):
    first = pl.program_id(0) < n_first
    pick = lambda a_ref, b_ref: jnp.where(first, a_ref[...], b_ref[...])
    gnw = gnw_ref[...]
    o = pick(oa_ref, ob_ref)
    for h in range(HG_HEADS):
        hs = slice(h * HG_DIM, (h + 1) * HG_DIM)
        oh = o[:, hs]
        ms = jnp.mean(oh * oh, axis=-1, keepdims=True)
        a_scr[:, hs] = (oh * lax.rsqrt(ms + EPS) * gnw * sg_ref[:, hs]).astype(BF16)
    br_a = _dot(a_scr[...], wa_ref[...])
    y = _gelu_exact(pick(ya_ref, yb_ref))
    y = y * _sigmoid(_dot(y.astype(BF16), wglu_ref[...]) + bglu_ref[...])
    br_b = _dot(y.astype(BF16), wb_ref[...])
    mixed = sa_ref[...] * br_a + sb_ref[...] * br_b
    out_ref[...] = pick(xa_ref, xb_ref) + _dot(mixed.astype(BF16), wout_ref[...])


def _post(xa, xb, oa, ob, ya, yb, proj, g_norm_w, w_glu, b_glu, w_a, w_b, w_out, tm):
    t = xa.shape[0] + xb.shape[0]
    n_first = xa.shape[0] // tm
    assert xa.shape[0] % tm == 0 and xb.shape[0] % tm == 0 and proj.shape[1] == t
    spec_a, spec_b = _two_source_specs(tm, n_first)
    slot = lambda s: pl.BlockSpec((None, tm, D_MODEL), lambda i: (s, i, 0))
    full = lambda shape: pl.BlockSpec(shape, lambda i: (0,) * len(shape))
    sq = (D_MODEL, D_MODEL)
    return pl.pallas_call(
        functools.partial(_post_body, n_first=n_first),
        grid=(t // tm,),
        in_specs=[spec_a, spec_b, spec_a, spec_b, spec_a, spec_b, slot(3), slot(5), slot(6),
                  full((1, HG_DIM)), full(sq), full((1, D_MODEL)), full(sq), full(sq), full(sq)],
        out_specs=pl.BlockSpec((tm, D_MODEL), lambda i: (i, 0)),
        out_shape=jax.ShapeDtypeStruct((t, D_MODEL), F32),
        scratch_shapes=[pltpu.VMEM((tm, D_MODEL), BF16)],
        compiler_params=_params("arbitrary"),
        name="post",
    )(xa, xb, oa, ob, ya, yb, proj, proj, proj, g_norm_w.reshape(1, HG_DIM), w_glu.astype(BF16),
      b_glu.reshape(1, D_MODEL), w_a.astype(BF16), w_b.astype(BF16), w_out.astype(BF16))


def _top_rows(w, count):
    rows = []
    for _ in range(count):
        m = jnp.max(w, axis=0, keepdims=True)
        rows.append(m)
        w = jnp.where(w == m, -jnp.inf, w)
    return rows


def _peer_body(x_ref, n2_ref, fn_ref, wq_ref, k1_ref, k2_ref, u_ref, vt_ref, ya_ref, yb_ref,
               hn_scr, s_scr, p_scr, tau_scr, vs_scr, act_scr, g_scr, acc_scr, *, tm, eb, n_first):
    i = pl.program_id(0)
    j = pl.program_id(1)
    nlt = tm // LANES
    nrow = eb // PEER_NKEYS
    assert nrow == SUBLANES

    @pl.when(j == 0)
    def _():
        x = x_ref[...]
        ms = jnp.mean(x * x, axis=-1, keepdims=True)
        hn_scr[...] = (x * lax.rsqrt(ms + EPS) * n2_ref[...]).astype(BF16)
        hn = hn_scr[...]
        for h in range(PEER_HEADS):
            for half, k_ref in enumerate((k1_ref, k2_ref)):
                c0 = (2 * h + half) * PEER_HALF
                qh = _dot(hn, wq_ref[:, c0:c0 + PEER_HALF]).astype(BF16)
                sc = _dot_nt(k_ref[h], qh)
                for lt in range(nlt):
                    s_scr[2 * h + half, lt] = sc[:, lt * LANES:(lt + 1) * LANES]

        def top_body(idx, carry):
            hh = idx // nlt
            lt = idx % nlt
            vs_scr[hh, lt] = jnp.concatenate(_top_rows(s_scr[hh, lt], PEER_TOPK), axis=0)
            return carry

        lax.fori_loop(0, 2 * PEER_HEADS * nlt, top_body, 0)

        def gate_body(idx, carry):
            h = idx // nlt
            lt = idx % nlt
            v1 = vs_scr[2 * h, lt]
            v2 = vs_scr[2 * h + 1, lt]
            cand = jnp.concatenate([v1[a:a + 1] + v2 for a in range(PEER_TOPK)], axis=0)
            best = _top_rows(cand, PEER_TOPK + 1)
            c16 = best[PEER_TOPK - 1]
            top = v1[0:1] + v2[0:1]
            z = jnp.sum(jnp.where(cand >= c16, jnp.exp(cand - top), 0.0), axis=0, keepdims=True)
            tau = 0.5 * (c16 + best[PEER_TOPK])
            tau_scr[h, lt] = jnp.broadcast_to(tau, (SUBLANES, LANES))
            s1 = s_scr[2 * h, lt]
            s2 = s_scr[2 * h + 1, lt]
            p_scr[2 * h, lt] = jnp.where(s1 >= v1[PEER_TOPK - 1:], jnp.exp(s1 - v1[0:1]), 0.0) / z
            p_scr[2 * h + 1, lt] = jnp.where(s2 >= v2[PEER_TOPK - 1:], jnp.exp(s2 - v2[0:1]), 0.0)
            return carry

        lax.fori_loop(0, PEER_HEADS * nlt, gate_body, 0)

    act_scr[...] = _dot_nt(u_ref[...], hn_scr[...])
    e1_rows = pl.ds(pl.multiple_of(j * nrow, nrow), nrow)

    def w_body(lt, carry):
        cols = pl.ds(pl.multiple_of(lt * LANES, LANES), LANES)
        for r in range(nrow):
            w = jnp.zeros((PEER_NKEYS, LANES), F32)
            for h in range(PEER_HEADS):
                theta = tau_scr[h, lt][0:1] - s_scr[2 * h, lt, e1_rows, :][r:r + 1]
                p1row = p_scr[2 * h, lt, e1_rows, :][r:r + 1]
                w = w + jnp.where(s_scr[2 * h + 1, lt] >= theta, p_scr[2 * h + 1, lt], 0.0) * p1row
            rows = slice(r * PEER_NKEYS, (r + 1) * PEER_NKEYS)
            g_scr[rows, cols] = (w * _gelu_exact(act_scr[rows, cols])).astype(BF16)
        return carry

    lax.fori_loop(0, nlt, w_body, 0)
    contrib = _dot(vt_ref[...], g_scr[...])

    @pl.when(j == 0)
    def _():
        acc_scr[...] = contrib

    @pl.when(j > 0)
    def _():
        acc_scr[...] += contrib

    def finish(y_ref):
        xo = x_ref[...] + acc_scr[...].T
        ms = jnp.mean(xo * xo, axis=-1, keepdims=True)
        y_ref[...] = xo * lax.rsqrt(ms + EPS) * fn_ref[...]

    last = j == pl.num_programs(1) - 1

    @pl.when(last & (i < n_first))
    def _():
        finish(ya_ref)

    @pl.when(last & (i >= n_first))
    def _():
        finish(yb_ref)


def _peer(x, n_first_rows, norm2_w, final_norm_w, wq, k1, k2, u_tab, v_tab, tm, eb):
    t = x.shape[0]
    assert t % tm == 0 and n_first_rows % tm == 0
    n_first = n_first_rows // tm
    nexp = u_tab.shape[0]
    assert nexp == PEER_NKEYS * PEER_NKEYS and nexp % eb == 0
    qw = 2 * PEER_HEADS * PEER_HALF
    nlt = tm // LANES
    full = lambda shape: pl.BlockSpec(shape, lambda i, j: (0,) * len(shape))
    out_a, out_b = _two_source_specs(tm, n_first)
    return pl.pallas_call(
        functools.partial(_peer_body, tm=tm, eb=eb, n_first=n_first),
        grid=(t // tm, nexp // eb),
        in_specs=[
            pl.BlockSpec((tm, D_MODEL), lambda i, j: (i, 0)),
            full((1, D_MODEL)), full((1, D_MODEL)),
            full((D_MODEL, qw)),
            full((PEER_HEADS, PEER_NKEYS, PEER_HALF)), full((PEER_HEADS, PEER_NKEYS, PEER_HALF)),
            pl.BlockSpec((eb, D_MODEL), lambda i, j: (j, 0)),
            pl.BlockSpec((D_MODEL, eb), lambda i, j: (0, j)),
        ],
        out_specs=[out_a, out_b],
        out_shape=[jax.ShapeDtypeStruct((n_first_rows, D_MODEL), F32),
                   jax.ShapeDtypeStruct((t - n_first_rows, D_MODEL), F32)],
        scratch_shapes=[
            pltpu.VMEM((tm, D_MODEL), BF16),
            pltpu.VMEM((2 * PEER_HEADS, nlt, PEER_NKEYS, LANES), F32),
            pltpu.VMEM((2 * PEER_HEADS, nlt, PEER_NKEYS, LANES), F32),
            pltpu.VMEM((PEER_HEADS, nlt, SUBLANES, LANES), F32),
            pltpu.VMEM((2 * PEER_HEADS, nlt, PEER_TOPK, LANES), F32),
            pltpu.VMEM((eb, tm), F32),
            pltpu.VMEM((eb, tm), BF16),
            pltpu.VMEM((D_MODEL, tm), F32),
        ],
        compiler_params=_params("arbitrary", "arbitrary"),
        name="peer",
    )(x, norm2_w.reshape(1, D_MODEL), final_norm_w.reshape(1, D_MODEL), wq.astype(BF16),
      k1.astype(BF16), k2.astype(BF16), u_tab.astype(BF16), v_tab.astype(BF16).T)


def kernel(x_prompt, x_sample, state_hgrn, state_ssm_re, state_ssm_im, meta_tokens, lower_bounds,
           norm1_w, w_in, g_norm_w, ssm_a_re, ssm_a_im, ssm_log_step, ssm_b_re, ssm_b_im,
           ssm_c_re, ssm_c_im, ssm_d, w_glu, b_glu, w_branch_a, w_branch_b, w_out, norm2_w,
           peer_wq, peer_k1, peer_k2, peer_u, peer_v, final_norm_w):
    depth = w_in.shape[0]
    assert depth == 1, "single-layer step only"
    nb, seq, _ = x_prompt.shape
    ns, dseq, _ = x_sample.shape
    tp = nb * seq
    tsm = ns * dseq
    tm = 512
    hg_c = 128
    small_c = 16
    s5_tok = 8
    s5_seq = 4
    assert tp % tm == 0 and tsm % tm == 0 and seq % hg_c == 0 and seq % s5_tok == 0
    assert nb % s5_seq == 0 and dseq <= small_c and dseq % 2 == 0 and N_META % s5_tok == 0

    w_in_b = w_in[0].astype(BF16)
    xp = x_prompt.reshape(tp, D_MODEL).astype(F32)
    xs = x_sample.reshape(tsm, D_MODEL).astype(F32)
    proj, lg = _inproj(xp, xs, norm1_w[0], lower_bounds, w_in_b, tm)
    proj_m, lg_m = _inproj(meta_tokens.astype(F32), None, norm1_w[0], lower_bounds, w_in_b, N_META)

    zero_hg = jnp.zeros((1, HG_HEADS, HG_DIM, HG_DIM), F32)
    _, hg_meta = _hgrn(proj_m, lg_m, zero_hg, 1, 1, N_META, 0)
    o_p, hg_p = _hgrn(proj, lg, hg_meta, nb, seq // hg_c, hg_c, 0)
    o_s, hg_s = _hgrn(proj, lg, state_hgrn[0].astype(F32), ns, 1, small_c, tp, ntok=dseq)
    o_s = o_s.reshape(tsm, D_MODEL)

    ssm = (ssm_a_re[0], ssm_a_im[0], ssm_log_step[0], ssm_b_re[0], ssm_b_im[0], ssm_c_re[0],
           ssm_c_im[0], ssm_d[0])
    ops_p = _s5_operators(*ssm, s5_tok)
    ops_s = _s5_operators(*ssm, dseq)
    nst = SSM_G * SSM_P
    u_meta = jnp.pad(proj_m[4], ((0, 7 * N_META), (0, 0)))
    zero_ss = jnp.zeros((1, SUBLANES, nst), F32)
    _, mre, mim = _s5(u_meta, 0, ops_p, zero_ss, zero_ss, 0, 1, SUBLANES, N_META // s5_tok, s5_tok, 3)
    nblk = nb // s5_seq
    x0re = jnp.broadcast_to(mre[:, 0:1], (nblk, SUBLANES, nst))
    x0im = jnp.broadcast_to(mim[:, 0:1], (nblk, SUBLANES, nst))
    y_p, pre, pim = _s5(proj, 4, ops_p, x0re, x0im, 0, nblk, s5_seq, seq // s5_tok, s5_tok, 2)
    y_s, sre, sim = _s5(proj, 4, ops_s, state_ssm_re[0].astype(F32).reshape(1, ns, nst),
                        state_ssm_im[0].astype(F32).reshape(1, ns, nst), tp, 1, ns, 1, dseq, 3)

    x1 = _post(xp, xs, o_p, o_s, y_p, y_s, proj, g_norm_w[0], w_glu[0], b_glu[0], w_branch_a[0],
               w_branch_b[0], w_out[0], tm)
    y_pr, y_sm = _peer(x1, tp, norm2_w[0], final_norm_w, peer_wq[0], peer_k1[0], peer_k2[0],
                       peer_u[0], peer_v[0], tm, SUBLANES * PEER_NKEYS)

    sd = state_hgrn.dtype
    st = lambda a: a[:, :s5_seq].reshape(nb, SSM_G, SSM_P)[None]
    return (y_pr.reshape(nb, seq, D_MODEL).astype(x_prompt.dtype),
            y_sm.reshape(ns, dseq, D_MODEL).astype(x_sample.dtype),
            hg_p[None].astype(sd),
            st(pre).astype(state_ssm_re.dtype),
            st(pim).astype(state_ssm_im.dtype),
            hg_s[None].astype(sd),
            sre.reshape(1, ns, SSM_G, SSM_P).astype(state_ssm_re.dtype),
            sim.reshape(1, ns, SSM_G, SSM_P).astype(state_ssm_im.dtype))
```

```python
import functools
import math

import numpy as np
import jax
import jax.numpy as jnp
from jax import lax
from jax.experimental import pallas as pl
from jax.experimental.pallas import tpu as pltpu

F32 = jnp.float32
BF16 = jnp.bfloat16

D_MODEL = 1024
N_META = 16
HG_HEADS = 8
HG_DIM = 128
SSM_G = 64
SSM_GROUP = 16
SSM_P = 64
SSM_MIN_RE = -1e-4
S5_GB = 8
PEER_HEADS = 8
PEER_NKEYS = 128
PEER_TOPK = 16
PEER_HALF = 128
EPS = 1e-6
LANES = 128
SUBLANES = 8
VMEM_LIMIT = 56 * 1024 * 1024


def _dot(a, b):
    return jnp.dot(a, b, preferred_element_type=F32)


def _dot_nt(a, b):
    return lax.dot_general(a, b, (((1,), (1,)), ((), ())), preferred_element_type=F32)


def _dot_tn(a, b):
    return lax.dot_general(a, b, (((0,), (0,)), ((), ())), preferred_element_type=F32)


def _split_bf16(x):
    hi = x.astype(BF16)
    lo = (x - hi.astype(F32)).astype(BF16)
    return hi, lo


def _sigmoid(x):
    return 1.0 / (1.0 + jnp.exp(-x))


def _gelu_exact(x):
    return 0.5 * x * (1.0 + lax.erf(x * (1.0 / math.sqrt(2.0))))


def _params(*sem):
    return pltpu.CompilerParams(dimension_semantics=sem, vmem_limit_bytes=VMEM_LIMIT)


def _two_source_specs(tm, n_first, ncols=D_MODEL):
    first = pl.BlockSpec((tm, ncols), lambda i, *_: (jnp.minimum(i, n_first - 1), 0))
    second = pl.BlockSpec((tm, ncols), lambda i, *_: (jnp.maximum(i - n_first, 0), 0))
    return first, second


N_SEG = 7


def _inproj_body(xa_ref, xb_ref, nw_ref, lbs_ref, w_ref, o_ref, lg_ref, h_scr, *, n_first):
    i = pl.program_id(0)
    j = pl.program_id(1)

    def normalise(x_ref):
        x = x_ref[...]
        ms = jnp.mean(x * x, axis=-1, keepdims=True)
        h_scr[...] = (x * lax.rsqrt(ms + EPS) * nw_ref[...]).astype(BF16)

    @pl.when((j == 0) & (i < n_first))
    def _():
        normalise(xa_ref)

    @pl.when((j == 0) & (i >= n_first))
    def _():
        normalise(xb_ref)

    p = _dot(h_scr[...], w_ref[...])

    @pl.when((j == 0) | (j == 3))
    def _():
        o_ref[...] = p * _sigmoid(p)

    @pl.when(j == 1)
    def _():
        lbs = lbs_ref[...]
        e = jnp.exp(lbs - jnp.max(lbs, axis=0, keepdims=True))
        lb = e[0:1] / jnp.sum(e, axis=0, keepdims=True)
        fg = lb + (1.0 - lb) * _sigmoid(p)
        o_ref[...] = 1.0 - fg
        lg_ref[...] = jnp.log(fg)

    @pl.when((j == 2) | (j == 4))
    def _():
        o_ref[...] = p

    @pl.when(j >= 5)
    def _():
        o_ref[...] = _sigmoid(p)


def _inproj(xa, xb, norm_w, lower_bounds, w_in_bf16, tm):
    if xb is None:
        xb = xa
        t = xa.shape[0]
        n_first = t // tm
    else:
        t = xa.shape[0] + xb.shape[0]
        n_first = xa.shape[0] // tm
        assert xa.shape[0] % tm == 0 and xb.shape[0] % tm == 0
    spec_a, spec_b = _two_source_specs(tm, n_first)
    return pl.pallas_call(
        functools.partial(_inproj_body, n_first=n_first),
        grid=(t // tm, N_SEG),
        in_specs=[
            spec_a, spec_b,
            pl.BlockSpec((1, D_MODEL), lambda i, j: (0, 0)),
            pl.BlockSpec(lower_bounds.shape, lambda i, j: (0, 0)),
            pl.BlockSpec((D_MODEL, D_MODEL), lambda i, j: (0, j)),
        ],
        out_specs=[
            pl.BlockSpec((None, tm, D_MODEL), lambda i, j: (j, i, 0)),
            pl.BlockSpec((tm, D_MODEL), lambda i, j: (i, 0)),
        ],
        out_shape=[
            jax.ShapeDtypeStruct((N_SEG, t, D_MODEL), F32),
            jax.ShapeDtypeStruct((t, D_MODEL), F32),
        ],
        scratch_shapes=[pltpu.VMEM((tm, D_MODEL), BF16)],
        compiler_params=_params("arbitrary", "arbitrary"),
        name="inproj",
    )(xa, xb, norm_w.reshape(1, D_MODEL), lower_bounds, w_in_bf16)


def _hgrn_consts(c):
    nlev = int(round(math.log2(c)))
    assert 1 << nlev == c
    d = np.zeros(((nlev + 2) * c, c), np.float32)
    for t in range(c):
        d[t, :t + 1] = 1.0
        d[c + t, t + 1:] = 1.0
        for l in range(1, nlev + 1):
            blk = 1 << l
            half = blk >> 1
            pos = t % blk
            m = t - pos + half
            row = (1 + l) * c + t
            if pos >= half:
                d[row, m:t + 1] = 1.0
            else:
                d[row, t + 1:m] = 1.0
    lev = np.full((c, c), -1, np.int32)
    for t in range(c):
        for s in range(t + 1):
            lev[t, s] = (t ^ s).bit_length()
    return d, lev, nlev


def _hgrn_body(q_ref, k_ref, v_ref, lg_ref, s0_ref, d_ref, lev_ref, o_ref, sout_ref, s_scr,
               *pad_scr, c, nlev, ntok, nsub):
    n = pl.program_id(0)
    r = pl.program_id(1)

    @pl.when(r == 0)
    def _():
        for sub in range(nsub):
            s_scr[sub] = s0_ref[sub if s0_ref.shape[0] == nsub else 0]

    if ntok < c:
        (pad,) = pad_scr

        @pl.when((n == 0) & (r == 0))
        def _():
            pad[...] = jnp.zeros_like(pad)

    dmat = d_ref[...]
    lev = lev_ref[...]
    ones = jnp.ones((c, HG_DIM), BF16)
    tidx = lax.broadcasted_iota(jnp.int32, (c, HG_DIM), 0)
    for sub in range(nsub):
        rows = slice(sub * ntok, (sub + 1) * ntok)
        if ntok < c:
            for a, ref in enumerate((q_ref, k_ref, v_ref, lg_ref)):
                pad[sub, a, 0:ntok, :] = ref[rows, :]
            qr, kr, vr, lr = (pad.at[sub, a] for a in range(4))
        else:
            qr, kr, vr, lr = q_ref, k_ref, v_ref, lg_ref
        for h in range(HG_HEADS):
            hs = slice(h * HG_DIM, (h + 1) * HG_DIM)
            q = qr[:, hs]
            k = kr[:, hs]
            vb = vr[:, hs].astype(BF16)
            lg_hi, lg_lo = _split_bf16(lr[:, hs])
            x = jnp.exp(_dot(dmat, lg_hi) + _dot(dmat, lg_lo))
            s_old = s_scr[sub, h]
            qd = (q * x[0:c]).astype(BF16)
            kd = (k * x[c:2 * c]).astype(BF16)
            o = _dot(qd, s_old.astype(BF16))
            att = jnp.where(lev == 0, _dot_nt(q.astype(BF16), k.astype(BF16)), 0.0)
            for l in range(1, nlev + 1):
                upper = ((tidx >> (l - 1)) & 1) == 1
                m = (jnp.where(upper, q, k) * x[(1 + l) * c:(2 + l) * c]).astype(BF16)
                att = jnp.where(lev == l, _dot_nt(m, m), att)
            o = o + _dot(att.astype(BF16), vb)
            o_ref[rows, hs] = o[0:ntok]
            gl = _dot_tn(lg_hi, ones) + _dot_tn(lg_lo, ones)
            s_scr[sub, h] = jnp.exp(gl) * s_old + _dot_tn(kd, vb)

    @pl.when(r == pl.num_programs(1) - 1)
    def _():
        sout_ref[...] = s_scr[...]


def _hgrn(proj, lg, s0, nseq, nchunks, c, row0, ntok=None):
    ntok = c if ntok is None else ntok
    dnp, levnp, nlev = _hgrn_consts(c)
    assert ntok == c or nchunks == 1
    nsub = 1 if ntok == c else SUBLANES // ntok
    step_rows = nsub * ntok
    assert row0 % step_rows == 0 and nseq % nsub == 0 and step_rows % SUBLANES == 0
    blk0 = row0 // step_rows
    tok_specs = [pl.BlockSpec((None, step_rows, D_MODEL),
                              functools.partial(lambda n, r, s: (s, blk0 + n * nchunks + r, 0), s=s))
                 for s in range(3)]
    lg_spec = pl.BlockSpec((step_rows, D_MODEL), lambda n, r: (blk0 + n * nchunks + r, 0))
    o_spec = pl.BlockSpec((step_rows, D_MODEL), lambda n, r: (n * nchunks + r, 0))
    o_shape = jax.ShapeDtypeStruct((nseq * nchunks * ntok, D_MODEL), F32)
    scratch = [] if ntok == c else [pltpu.VMEM((nsub, 4, c, D_MODEL), F32)]
    st_blk = (nsub, HG_HEADS, HG_DIM, HG_DIM)
    if s0.shape[0] == 1:
        s0_spec = pl.BlockSpec((1,) + st_blk[1:], lambda n, r: (0, 0, 0, 0))
    else:
        s0_spec = pl.BlockSpec(st_blk, lambda n, r: (n, 0, 0, 0))
    return pl.pallas_call(
        functools.partial(_hgrn_body, c=c, nlev=nlev, ntok=ntok, nsub=nsub),
        grid=(nseq // nsub, nchunks),
        in_specs=tok_specs + [
            lg_spec, s0_spec,
            pl.BlockSpec(dnp.shape, lambda n, r: (0, 0)),
            pl.BlockSpec(levnp.shape, lambda n, r: (0, 0)),
        ],
        out_specs=[o_spec, pl.BlockSpec(st_blk, lambda n, r: (n, 0, 0, 0))],
        out_shape=[o_shape, jax.ShapeDtypeStruct((nseq, HG_HEADS, HG_DIM, HG_DIM), F32)],
        scratch_shapes=[pltpu.VMEM(st_blk, F32)] + scratch,
        compiler_params=_params("arbitrary", "arbitrary"),
        name=f"hgrn_c{c}_t{ntok}",
    )(proj, proj, proj, lg, s0, jnp.asarray(dnp, BF16), jnp.asarray(levnp))


def _s5_operators(a_re, a_im, log_step, b_re, b_im, c_re, c_im, d_skip, ntok):
    hp = lax.Precision.HIGHEST
    nb = SSM_G // S5_GB
    lam_re = jnp.minimum(a_re.astype(F32), SSM_MIN_RE)
    lam_im = a_im.astype(F32)
    delta = jnp.exp(log_step.astype(F32))[:, None]
    n = jnp.arange(ntok + 1, dtype=F32)[:, None, None]
    mag = jnp.exp(lam_re * delta * n)
    ang = lam_im * delta * n
    pw_re = mag * jnp.cos(ang)
    pw_im = mag * jnp.sin(ang)
    x = pw_re[1] - 1.0
    y = pw_im[1]
    den = lam_re * lam_re + lam_im * lam_im
    cr = (x * lam_re + y * lam_im) / den
    ci = (y * lam_re - x * lam_im) / den
    bb_re = cr[..., None] * b_re.astype(F32) - ci[..., None] * b_im.astype(F32)
    bb_im = cr[..., None] * b_im.astype(F32) + ci[..., None] * b_re.astype(F32)
    w_re = pw_re[:ntok, :, :, None] * bb_re[None] - pw_im[:ntok, :, :, None] * bb_im[None]
    w_im = pw_re[:ntok, :, :, None] * bb_im[None] + pw_im[:ntok, :, :, None] * bb_re[None]
    cc_re = c_re.astype(F32)
    cc_im = c_im.astype(F32)
    kl = (jnp.einsum('gcp,ngpd->ngdc', cc_re, w_re, precision=hp)
          - jnp.einsum('gcp,ngpd->ngdc', cc_im, w_im, precision=hp))
    kl = kl.at[0].add(d_skip.astype(F32).reshape(SSM_G, SSM_GROUP)[:, :, None]
                      * jnp.eye(SSM_GROUP, dtype=F32)[None])
    zero = jnp.zeros_like(kl[0])
    base = jnp.stack([jnp.stack([kl[t - s] if t >= s else zero for t in range(ntok)], axis=0)
                      for s in range(ntok)], axis=0)
    eye_g = jnp.eye(S5_GB, dtype=F32)
    base = base.reshape(ntok, ntok, nb, S5_GB, SSM_GROUP, SSM_GROUP)
    base = jnp.transpose(base, (2, 0, 3, 4, 1, 5))
    tz = base[:, :, :, :, :, None, :] * eye_g[None, None, :, None, None, :, None]
    tz = tz.reshape(nb, ntok, LANES, ntok * LANES)

    def state_in(w):
        w = jnp.flip(w, axis=0).reshape(ntok, nb, S5_GB, SSM_P, SSM_GROUP)
        w = jnp.transpose(w, (1, 0, 2, 4, 3))
        w = w[:, :, :, :, None, :] * eye_g[None, None, :, None, :, None]
        return w.reshape(nb, ntok, LANES, S5_GB * SSM_P)

    def state_out(m_re, m_im, sign):
        p_re = pw_re[1:ntok + 1]
        p_im = pw_im[1:ntok + 1]
        if sign > 0:
            w = m_re[None] * p_re[:, :, None, :] - m_im[None] * p_im[:, :, None, :]
        else:
            w = -(m_re[None] * p_im[:, :, None, :] + m_im[None] * p_re[:, :, None, :])
        w = w.reshape(ntok, nb, S5_GB, SSM_GROUP, SSM_P)
        w = jnp.transpose(w, (1, 2, 4, 0, 3))
        w = w[:, :, :, :, None, :] * eye_g[None, :, None, None, :, None]
        return w.reshape(nb, S5_GB * SSM_P, ntok * LANES)

    apow = jnp.stack([pw_re[ntok].reshape(nb, S5_GB * SSM_P),
                      pw_im[ntok].reshape(nb, S5_GB * SSM_P)], axis=1)
    return dict(tz=tz.astype(BF16), wst_re=state_in(w_re), wst_im=state_in(w_im),
                wo_re=state_out(cc_re, cc_im, +1).astype(BF16),
                wo_im=state_out(cc_re, cc_im, -1).astype(BF16), apow=apow)


def _s5_body(u_ref, tz_ref, wre_ref, wim_ref, wore_ref, woim_ref, a_ref, x0re_ref, x0im_ref,
             y_ref, xre_ref, xim_ref, zre_scr, zim_scr, ub_scr,
             *, ntok, nchunks, nseq, npass):
    rows = nseq * nchunks
    nlb = (S5_GB * SSM_P) // LANES
    lane_blk = lambda lb: slice(lb * LANES, (lb + 1) * LANES)
    for s in range(ntok):
        us = u_ref[pl.ds(s, rows, stride=ntok), :]
        uh, ul = _split_bf16(us)
        ub_scr[s] = uh
        for w_ref, z_scr in ((wre_ref, zre_scr), (wim_ref, zim_scr)):
            wh, wl = _split_bf16(w_ref[s])
            z = _dot(uh, wh)
            if npass >= 2:
                z = z + _dot(ul, wh)
            if npass >= 3:
                z = z + _dot(uh, wl)
            for lb in range(nlb):
                if s == 0:
                    z_scr[lb] = z[:, lane_blk(lb)]
                else:
                    z_scr[lb] += z[:, lane_blk(lb)]
    a_re = a_ref[0:1, :]
    a_im = a_ref[1:2, :]

    def step(r, carry):
        x_re, x_im = carry
        at = pl.ds(r, nseq, stride=nchunks) if nchunks > 1 else pl.ds(0, nseq)
        z_re = jnp.concatenate([zre_scr[lb, at, :] for lb in range(nlb)], axis=1)
        z_im = jnp.concatenate([zim_scr[lb, at, :] for lb in range(nlb)], axis=1)
        for lb in range(nlb):
            zre_scr[lb, at, :] = x_re[:, lane_blk(lb)]
            zim_scr[lb, at, :] = x_im[:, lane_blk(lb)]
        return (a_re * x_re - a_im * x_im + z_re, a_re * x_im + a_im * x_re + z_im)

    x_re, x_im = lax.fori_loop(0, nchunks, step, (x0re_ref[0:nseq, :], x0im_ref[0:nseq, :]))
    xre_ref[...] = jnp.zeros_like(xre_ref)
    xim_ref[...] = jnp.zeros_like(xim_ref)
    xre_ref[0:nseq, :] = x_re
    xim_ref[0:nseq, :] = x_im
    for t0 in range(0, ntok, 2):
        cols = slice(t0 * LANES, (t0 + 2) * LANES)
        acc = jnp.zeros((rows, 2 * LANES), F32)
        for lb in range(nlb):
            acc = acc + _dot(zre_scr[lb].astype(BF16), wore_ref[lane_blk(lb), cols])
            acc = acc + _dot(zim_scr[lb].astype(BF16), woim_ref[lane_blk(lb), cols])
        for s in range(t0 + 2):
            acc = acc + _dot(ub_scr[s], tz_ref[s, :, cols])
        y_ref[pl.ds(t0, rows, stride=ntok), :] = acc[:, :LANES]
        y_ref[pl.ds(t0 + 1, rows, stride=ntok), :] = acc[:, LANES:]


def _s5(u, slot, ops, x0_re, x0_im, row0, nblk, nseq, nchunks, ntok, npass):
    nb = SSM_G // S5_GB
    rows = nseq * nchunks
    trows = rows * ntok
    assert row0 % trows == 0 and ntok % 2 == 0
    blk0 = row0 // trows
    nseq_pad = x0_re.shape[1]
    lanes = S5_GB * SSM_P
    if u.ndim == 3:
        u_spec = pl.BlockSpec((None, trows, LANES), lambda g, sb: (slot, blk0 + sb, g))
    else:
        u_spec = pl.BlockSpec((trows, LANES), lambda g, sb: (blk0 + sb, g))
    per_g = lambda shape: pl.BlockSpec((None,) + shape, lambda g, sb: (g,) + (0,) * len(shape))
    st_spec = pl.BlockSpec((None, nseq_pad, lanes), lambda g, sb: (sb, 0, g))
    st_shape = jax.ShapeDtypeStruct((nblk, nseq_pad, SSM_G * SSM_P), F32)
    return pl.pallas_call(
        functools.partial(_s5_body, ntok=ntok, nchunks=nchunks, nseq=nseq, npass=npass),
        grid=(nb, nblk),
        in_specs=[
            u_spec,
            per_g((ntok, LANES, ntok * LANES)),
            per_g((ntok, LANES, lanes)), per_g((ntok, LANES, lanes)),
            per_g((lanes, ntok * LANES)), per_g((lanes, ntok * LANES)),
            per_g((2, lanes)),
            st_spec, st_spec,
        ],
        out_specs=[pl.BlockSpec((trows, LANES), lambda g, sb: (sb, g)), st_spec, st_spec],
        out_shape=[jax.ShapeDtypeStruct((nblk * trows, D_MODEL), F32), st_shape, st_shape],
        scratch_shapes=[pltpu.VMEM((lanes // LANES, rows, LANES), F32),
                        pltpu.VMEM((lanes // LANES, rows, LANES), F32),
                        pltpu.VMEM((ntok, rows, LANES), BF16)],
        compiler_params=_params("arbitrary", "arbitrary"),
        name=f"s5_r{rows}_t{ntok}",
    )(u, ops['tz'], ops['wst_re'], ops['wst_im'], ops['wo_re'], ops['wo_im'], ops['apow'],
      x0_re, x0_im)


def _post_body(xa_ref, xb_ref, oa_ref, ob_ref, ya_ref, yb_ref, sg_ref, sa_ref, sb_ref, gnw_ref,
               wglu_ref, bglu_ref, wa_ref, wb_ref, wout_ref, out_ref, a_scr, *, n_first):
    first = pl.program_id(0) < n_first
    pick = lambda a_ref, b_ref: jnp.where(first, a_ref[...], b_ref[...])
    gnw = gnw_ref[...]
    o = pick(oa_ref, ob_ref)
    for h in range(HG_HEADS):
        hs = slice(h * HG_DIM, (h + 1) * HG_DIM)
        oh = o[:, hs]
        ms = jnp.mean(oh * oh, axis=-1, keepdims=True)
        a_scr[:, hs] = (oh * lax.rsqrt(ms + EPS) * gnw * sg_ref[:, hs]).astype(BF16)
    br_a = _dot(a_scr[...], wa_ref[...])
    y = _gelu_exact(pick(ya_ref, yb_ref))
    y = y * _sigmoid(_dot(y.astype(BF16), wglu_ref[...]) + bglu_ref[...])
    br_b = _dot(y.astype(BF16), wb_ref[...])
    mixed = sa_ref[...] * br_a + sb_ref[...] * br_b
    out_ref[...] = pick(xa_ref, xb_ref) + _dot(mixed.astype(BF16), wout_ref[...])


def _post(xa, xb, oa, ob, ya, yb, proj, g_norm_w, w_glu, b_glu, w_a, w_b, w_out, tm):
    t = xa.shape[0] + xb.shape[0]
    n_first = xa.shape[0] // tm
    assert xa.shape[0] % tm == 0 and xb.shape[0] % tm == 0 and proj.shape[1] == t
    spec_a, spec_b = _two_source_specs(tm, n_first)
    slot = lambda s: pl.BlockSpec((None, tm, D_MODEL), lambda i: (s, i, 0))
    full = lambda shape: pl.BlockSpec(shape, lambda i: (0,) * len(shape))
    sq = (D_MODEL, D_MODEL)
    return pl.pallas_call(
        functools.partial(_post_body, n_first=n_first),
        grid=(t // tm,),
        in_specs=[spec_a, spec_b, spec_a, spec_b, spec_a, spec_b, slot(3), slot(5), slot(6),
                  full((1, HG_DIM)), full(sq), full((1, D_MODEL)), full(sq), full(sq), full(sq)],
        out_specs=pl.BlockSpec((tm, D_MODEL), lambda i: (i, 0)),
        out_shape=jax.ShapeDtypeStruct((t, D_MODEL), F32),
        scratch_shapes=[pltpu.VMEM((tm, D_MODEL), BF16)],
        compiler_params=_params("arbitrary"),
        name="post",
    )(xa, xb, oa, ob, ya, yb, proj, proj, proj, g_norm_w.reshape(1, HG_DIM), w_glu.astype(BF16),
      b_glu.reshape(1, D_MODEL), w_a.astype(BF16), w_b.astype(BF16), w_out.astype(BF16))


def _top_rows(w, count):
    rows = []
    for _ in range(count):
        m = jnp.max(w, axis=0, keepdims=True)
        rows.append(m)
        w = jnp.where(w == m, -jnp.inf, w)
    return rows


_CAND_COUNTS = tuple(PEER_TOPK // (a + 1) for a in range(SUBLANES))


def _peer_body(x_ref, n2_ref, fn_ref, wq_ref, k1_ref, k2_ref, u_ref, vt_ref, ya_ref, yb_ref,
               hnt_scr, s_scr, p_scr, tau_scr, vs_scr, act0, act1, g0, g1, acc_scr,
               *, tm, eb, n_first):
    i = pl.program_id(0)
    j = pl.program_id(1)
    nblk = pl.num_programs(1) - 2
    nlt = tm // LANES
    nrow = eb // PEER_NKEYS
    assert nrow == SUBLANES

    @pl.when(j == 0)
    def _():
        act1[...] = jnp.zeros_like(act1)
        g1[...] = jnp.zeros_like(g1)
        acc_scr[...] = jnp.zeros_like(acc_scr)
        x = x_ref[...]
        ms = jnp.mean(x * x, axis=-1, keepdims=True)
        hn32 = x * lax.rsqrt(ms + EPS) * n2_ref[...]
        hnt_scr[...] = hn32.T.astype(BF16)
        hn = hn32.astype(BF16)
        for h in range(PEER_HEADS):
            c0 = 2 * h * PEER_HALF
            q2 = _dot(hn, wq_ref[:, c0:c0 + 2 * PEER_HALF]).astype(BF16)
            for half, k_ref in enumerate((k1_ref, k2_ref)):
                sc = _dot_nt(k_ref[h], q2[:, half * PEER_HALF:(half + 1) * PEER_HALF])
                for lt in range(nlt):
                    s_scr[2 * h + half, lt] = sc[:, lt * LANES:(lt + 1) * LANES]

        def top_body(idx, carry):
            hh = idx // nlt
            lt = idx % nlt
            vs_scr[hh, lt] = jnp.concatenate(_top_rows(s_scr[hh, lt], PEER_TOPK), axis=0)
            return carry

        lax.fori_loop(0, 2 * PEER_HEADS * nlt, top_body, 0)

        def gate_body(idx, carry):
            h = idx // nlt
            lt = idx % nlt
            v1 = vs_scr[2 * h, lt]
            v2 = vs_scr[2 * h + 1, lt]
            rank = lax.broadcasted_iota(jnp.int32, (PEER_TOPK, LANES), 0)
            cand = jnp.concatenate(
                [jnp.where(rank < cnt, v1[a:a + 1] + v2, -jnp.inf) for a, cnt in enumerate(_CAND_COUNTS)]
                + [v1[SUBLANES:] + v2[0:1]], axis=0)
            best = _top_rows(cand, PEER_TOPK + 1)
            c16 = best[PEER_TOPK - 1]
            top = v1[0:1] + v2[0:1]
            z = jnp.sum(jnp.where(cand >= c16, jnp.exp(cand - top), 0.0), axis=0, keepdims=True)
            tau = 0.5 * (c16 + best[PEER_TOPK])
            tau_scr[h, lt] = jnp.broadcast_to(tau, (SUBLANES, LANES))
            s1 = s_scr[2 * h, lt]
            s2 = s_scr[2 * h + 1, lt]
            p_scr[2 * h, lt] = jnp.where(s1 >= v1[PEER_TOPK - 1:], jnp.exp(s1 - v1[0:1]), 0.0) / z
            p_scr[2 * h + 1, lt] = jnp.where(s2 >= v2[PEER_TOPK - 1:], jnp.exp(s2 - v2[0:1]), 0.0)
            return carry

        lax.fori_loop(0, PEER_HEADS * nlt, gate_body, 0)

    e1_rows = pl.ds(pl.multiple_of(jnp.clip(j - 1, 0, nblk - 1) * nrow, nrow), nrow)

    def stages(act_new, act_cur, g_new, g_cur):
        def pre_activation(rows):
            act_new[rows, :] = _dot(u_ref[rows, :], hnt_scr[...])

        def accumulate(rows):
            acc_scr[rows, :] += _dot(vt_ref[rows, :], g_cur[...])

        def gated(lt, r0, kh):
            cols = slice(lt * LANES, (lt + 1) * LANES)
            keys = slice(kh * half_keys, (kh + 1) * half_keys)
            ws = [jnp.zeros((half_keys, LANES), F32) for _ in range(2)]
            for h in range(PEER_HEADS):
                s2 = s_scr[2 * h + 1, lt, keys, :]
                p2 = p_scr[2 * h + 1, lt, keys, :]
                s1grp = s_scr[2 * h, lt, e1_rows, :]
                p1grp = p_scr[2 * h, lt, e1_rows, :]
                tau = tau_scr[h, lt][0:1]
                for d in range(2):
                    theta = tau - s1grp[r0 + d:r0 + d + 1]
                    ws[d] = ws[d] + jnp.where(s2 >= theta, p2, 0.0) * p1grp[r0 + d:r0 + d + 1]
            for d in range(2):
                rows = slice((r0 + d) * PEER_NKEYS + kh * half_keys, (r0 + d) * PEER_NKEYS + (kh + 1) * half_keys)
                g_new[rows, cols] = (ws[d] * _gelu_exact(act_cur[rows, cols])).astype(BF16)

        chunk = LANES
        matmul_jobs = ([functools.partial(pre_activation, slice(m, m + chunk)) for m in range(0, eb, chunk)]
                       + [functools.partial(accumulate, slice(m, m + chunk)) for m in range(0, D_MODEL, chunk)])
        half_keys = PEER_NKEYS // 2
        gated_jobs = [functools.partial(gated, lt, r0, kh)
                      for lt in range(nlt) for r0 in range(0, nrow, 2) for kh in range(2)]
        per = -(-len(gated_jobs) // len(matmul_jobs))
        for k, job in enumerate(matmul_jobs):
            job()
            for gjob in gated_jobs[k * per:(k + 1) * per]:
                gjob()

    @pl.when(j % 2 == 0)
    def _():
        stages(act0, act1, g0, g1)

    @pl.when(j % 2 == 1)
    def _():
        stages(act1, act0, g1, g0)

    def finish(y_ref):
        xo = x_ref[...] + acc_scr[...].T
        ms = jnp.mean(xo * xo, axis=-1, keepdims=True)
        y_ref[...] = xo * lax.rsqrt(ms + EPS) * fn_ref[...]

    last = j == pl.num_programs(1) - 1

    @pl.when(last & (i < n_first))
    def _():
        finish(ya_ref)

    @pl.when(last & (i >= n_first))
    def _():
        finish(yb_ref)


def _peer(x, n_first_rows, norm2_w, final_norm_w, wq, k1, k2, u_tab, v_tab, tm, eb):
    t = x.shape[0]
    assert t % tm == 0 and n_first_rows % tm == 0
    n_first = n_first_rows // tm
    nexp = u_tab.shape[0]
    assert nexp == PEER_NKEYS * PEER_NKEYS and nexp % eb == 0
    nblk = nexp // eb
    qw = 2 * PEER_HEADS * PEER_HALF
    nlt = tm // LANES
    full = lambda shape: pl.BlockSpec(shape, lambda i, j: (0,) * len(shape))
    out_a, out_b = _two_source_specs(tm, n_first)
    return pl.pallas_call(
        functools.partial(_peer_body, tm=tm, eb=eb, n_first=n_first),
        grid=(t // tm, nblk + 2),
        in_specs=[
            pl.BlockSpec((tm, D_MODEL), lambda i, j: (i, 0)),
            full((1, D_MODEL)), full((1, D_MODEL)),
            full((D_MODEL, qw)),
            full((PEER_HEADS, PEER_NKEYS, PEER_HALF)), full((PEER_HEADS, PEER_NKEYS, PEER_HALF)),
            pl.BlockSpec((eb, D_MODEL), lambda i, j: (jnp.minimum(j, nblk - 1), 0)),
            pl.BlockSpec((D_MODEL, eb), lambda i, j: (0, jnp.clip(j - 2, 0, nblk - 1))),
        ],
        out_specs=[out_a, out_b],
        out_shape=[jax.ShapeDtypeStruct((n_first_rows, D_MODEL), F32),
                   jax.ShapeDtypeStruct((t - n_first_rows, D_MODEL), F32)],
        scratch_shapes=[
            pltpu.VMEM((D_MODEL, tm), BF16),
            pltpu.VMEM((2 * PEER_HEADS, nlt, PEER_NKEYS, LANES), F32),
            pltpu.VMEM((2 * PEER_HEADS, nlt, PEER_NKEYS, LANES), F32),
            pltpu.VMEM((PEER_HEADS, nlt, SUBLANES, LANES), F32),
            pltpu.VMEM((2 * PEER_HEADS, nlt, PEER_TOPK, LANES), F32),
            pltpu.VMEM((eb, tm), F32), pltpu.VMEM((eb, tm), F32),
            pltpu.VMEM((eb, tm), BF16), pltpu.VMEM((eb, tm), BF16),
            pltpu.VMEM((D_MODEL, tm), F32),
        ],
        compiler_params=_params("arbitrary", "arbitrary"),
        name="peer",
    )(x, norm2_w.reshape(1, D_MODEL), final_norm_w.reshape(1, D_MODEL), wq.astype(BF16),
      k1.astype(BF16), k2.astype(BF16), u_tab.astype(BF16), v_tab.astype(BF16).T)


def kernel(x_prompt, x_sample, state_hgrn, state_ssm_re, state_ssm_im, meta_tokens, lower_bounds,
           norm1_w, w_in, g_norm_w, ssm_a_re, ssm_a_im, ssm_log_step, ssm_b_re, ssm_b_im,
           ssm_c_re, ssm_c_im, ssm_d, w_glu, b_glu, w_branch_a, w_branch_b, w_out, norm2_w,
           peer_wq, peer_k1, peer_k2, peer_u, peer_v, final_norm_w):
    depth = w_in.shape[0]
    assert depth == 1, "single-layer step only"
    nb, seq, _ = x_prompt.shape
    ns, dseq, _ = x_sample.shape
    tp = nb * seq
    tsm = ns * dseq
    tm = 512
    hg_c = 128
    small_c = 16
    s5_tok = 8
    s5_seq = 4
    assert tp % tm == 0 and tsm % tm == 0 and seq % hg_c == 0 and seq % s5_tok == 0
    assert nb % s5_seq == 0 and dseq <= small_c and dseq % 2 == 0 and N_META % s5_tok == 0

    w_in_b = w_in[0].astype(BF16)
    xp = x_prompt.reshape(tp, D_MODEL).astype(F32)
    xs = x_sample.reshape(tsm, D_MODEL).astype(F32)
    proj, lg = _inproj(xp, xs, norm1_w[0], lower_bounds, w_in_b, tm)
    proj_m, lg_m = _inproj(meta_tokens.astype(F32), None, norm1_w[0], lower_bounds, w_in_b, N_META)

    zero_hg = jnp.zeros((1, HG_HEADS, HG_DIM, HG_DIM), F32)
    _, hg_meta = _hgrn(proj_m, lg_m, zero_hg, 1, 1, N_META, 0)
    o_p, hg_p = _hgrn(proj, lg, hg_meta, nb, seq // hg_c, hg_c, 0)
    o_s, hg_s = _hgrn(proj, lg, state_hgrn[0].astype(F32), ns, 1, small_c, tp, ntok=dseq)
    o_s = o_s.reshape(tsm, D_MODEL)

    ssm = (ssm_a_re[0], ssm_a_im[0], ssm_log_step[0], ssm_b_re[0], ssm_b_im[0], ssm_c_re[0],
           ssm_c_im[0], ssm_d[0])
    ops_p = _s5_operators(*ssm, s5_tok)
    ops_s = _s5_operators(*ssm, dseq)
    nst = SSM_G * SSM_P
    u_meta = jnp.pad(proj_m[4], ((0, 7 * N_META), (0, 0)))
    zero_ss = jnp.zeros((1, SUBLANES, nst), F32)
    _, mre, mim = _s5(u_meta, 0, ops_p, zero_ss, zero_ss, 0, 1, SUBLANES, N_META // s5_tok, s5_tok, 3)
    nblk = nb // s5_seq
    x0re = jnp.broadcast_to(mre[:, 0:1], (nblk, SUBLANES, nst))
    x0im = jnp.broadcast_to(mim[:, 0:1], (nblk, SUBLANES, nst))
    y_p, pre, pim = _s5(proj, 4, ops_p, x0re, x0im, 0, nblk, s5_seq, seq // s5_tok, s5_tok, 2)
    y_s, sre, sim = _s5(proj, 4, ops_s, state_ssm_re[0].astype(F32).reshape(1, ns, nst),
                        state_ssm_im[0].astype(F32).reshape(1, ns, nst), tp, 1, ns, 1, dseq, 3)

    x1 = _post(xp, xs, o_p, o_s, y_p, y_s, proj, g_norm_w[0], w_glu[0], b_glu[0], w_branch_a[0],
               w_branch_b[0], w_out[0], tm)
    y_pr, y_sm = _peer(x1, tp, norm2_w[0], final_norm_w, peer_wq[0], peer_k1[0], peer_k2[0],
                       peer_u[0], peer_v[0], tm, SUBLANES * PEER_NKEYS)

    sd = state_hgrn.dtype
    st = lambda a: a[:, :s5_seq].reshape(nb, SSM_G, SSM_P)[None]
    return (y_pr.reshape(nb, seq, D_MODEL).astype(x_prompt.dtype),
            y_sm.reshape(ns, dseq, D_MODEL).astype(x_sample.dtype),
            hg_p[None].astype(sd),
            st(pre).astype(state_ssm_re.dtype),
            st(pim).astype(state_ssm_im.dtype),
            hg_s[None].astype(sd),
            sre.reshape(1, ns, SSM_G, SSM_P).astype(state_ssm_re.dtype),
            sim.reshape(1, ns, SSM_G, SSM_P).astype(state_ssm_im.dtype))
```

```python
import functools
import math

import numpy as np
import jax
import jax.numpy as jnp
from jax import lax
from jax.experimental import pallas as pl
from jax.experimental.pallas import tpu as pltpu

F32 = jnp.float32
BF16 = jnp.bfloat16

D_MODEL = 1024
N_META = 16
HG_HEADS = 8
HG_DIM = 128
SSM_G = 64
SSM_GROUP = 16
SSM_P = 64
SSM_MIN_RE = -1e-4
S5_GB = 8
PEER_HEADS = 8
PEER_NKEYS = 128
PEER_TOPK = 16
PEER_HALF = 128
EPS = 1e-6
LANES = 128
SUBLANES = 8
VMEM_LIMIT = 56 * 1024 * 1024


def _dot(a, b):
    return jnp.dot(a, b, preferred_element_type=F32)


def _dot_nt(a, b):
    return lax.dot_general(a, b, (((1,), (1,)), ((), ())), preferred_element_type=F32)


def _dot_tn(a, b):
    return lax.dot_general(a, b, (((0,), (0,)), ((), ())), preferred_element_type=F32)


def _split_bf16(x):
    hi = x.astype(BF16)
    lo = (x - hi.astype(F32)).astype(BF16)
    return hi, lo


def _sigmoid(x):
    return 1.0 / (1.0 + jnp.exp(-x))


def _gelu_exact(x):
    return 0.5 * x * (1.0 + lax.erf(x * (1.0 / math.sqrt(2.0))))


def _params(*sem):
    return pltpu.CompilerParams(dimension_semantics=sem, vmem_limit_bytes=VMEM_LIMIT)


def _two_source_specs(tm, n_first, ncols=D_MODEL):
    first = pl.BlockSpec((tm, ncols), lambda i, *_: (jnp.minimum(i, n_first - 1), 0))
    second = pl.BlockSpec((tm, ncols), lambda i, *_: (jnp.maximum(i - n_first, 0), 0))
    return first, second


N_SEG = 7


def _inproj_body(xa_ref, xb_ref, nw_ref, lbs_ref, w_ref, o_ref, lg_ref, h_scr, *, n_first):
    i = pl.program_id(0)
    j = pl.program_id(1)

    def normalise(x_ref):
        x = x_ref[...]
        ms = jnp.mean(x * x, axis=-1, keepdims=True)
        h_scr[...] = (x * lax.rsqrt(ms + EPS) * nw_ref[...]).astype(BF16)

    @pl.when((j == 0) & (i < n_first))
    def _():
        normalise(xa_ref)

    @pl.when((j == 0) & (i >= n_first))
    def _():
        normalise(xb_ref)

    p = _dot(h_scr[...], w_ref[...])

    @pl.when((j == 0) | (j == 3))
    def _():
        o_ref[...] = p * _sigmoid(p)

    @pl.when(j == 1)
    def _():
        lbs = lbs_ref[...]
        e = jnp.exp(lbs - jnp.max(lbs, axis=0, keepdims=True))
        lb = e[0:1] / jnp.sum(e, axis=0, keepdims=True)
        fg = lb + (1.0 - lb) * _sigmoid(p)
        o_ref[...] = 1.0 - fg
        lg_ref[...] = jnp.log(fg)

    @pl.when((j == 2) | (j == 4))
    def _():
        o_ref[...] = p

    @pl.when(j >= 5)
    def _():
        o_ref[...] = _sigmoid(p)


def _inproj(xa, xb, norm_w, lower_bounds, w_in_bf16, tm):
    if xb is None:
        xb = xa
        t = xa.shape[0]
        n_first = t // tm
    else:
        t = xa.shape[0] + xb.shape[0]
        n_first = xa.shape[0] // tm
        assert xa.shape[0] % tm == 0 and xb.shape[0] % tm == 0
    spec_a, spec_b = _two_source_specs(tm, n_first)
    return pl.pallas_call(
        functools.partial(_inproj_body, n_first=n_first),
        grid=(t // tm, N_SEG),
        in_specs=[
            spec_a, spec_b,
            pl.BlockSpec((1, D_MODEL), lambda i, j: (0, 0)),
            pl.BlockSpec(lower_bounds.shape, lambda i, j: (0, 0)),
            pl.BlockSpec((D_MODEL, D_MODEL), lambda i, j: (0, j)),
        ],
        out_specs=[
            pl.BlockSpec((None, tm, D_MODEL), lambda i, j: (j, i, 0)),
            pl.BlockSpec((tm, D_MODEL), lambda i, j: (i, 0)),
        ],
        out_shape=[
            jax.ShapeDtypeStruct((N_SEG, t, D_MODEL), F32),
            jax.ShapeDtypeStruct((t, D_MODEL), F32),
        ],
        scratch_shapes=[pltpu.VMEM((tm, D_MODEL), BF16)],
        compiler_params=_params("arbitrary", "arbitrary"),
        name="inproj",
    )(xa, xb, norm_w.reshape(1, D_MODEL), lower_bounds, w_in_bf16)


def _hgrn_consts(c):
    nlev = int(round(math.log2(c)))
    assert 1 << nlev == c
    d = np.zeros(((nlev + 2) * c, c), np.float32)
    for t in range(c):
        d[t, :t + 1] = 1.0
        d[c + t, t + 1:] = 1.0
        for l in range(1, nlev + 1):
            blk = 1 << l
            half = blk >> 1
            pos = t % blk
            m = t - pos + half
            row = (1 + l) * c + t
            if pos >= half:
                d[row, m:t + 1] = 1.0
            else:
                d[row, t + 1:m] = 1.0
    lev = np.full((c, c), -1, np.int32)
    for t in range(c):
        for s in range(t + 1):
            lev[t, s] = (t ^ s).bit_length()
    return d, lev, nlev


def _hgrn_body(q_ref, k_ref, v_ref, lg_ref, s0_ref, d_ref, lev_ref, o_ref, sout_ref, s_scr,
               *pad_scr, c, nlev, ntok, nsub):
    n = pl.program_id(0)
    r = pl.program_id(1)

    @pl.when(r == 0)
    def _():
        for sub in range(nsub):
            s_scr[sub] = s0_ref[sub if s0_ref.shape[0] == nsub else 0]

    if ntok < c:
        (pad,) = pad_scr

        @pl.when((n == 0) & (r == 0))
        def _():
            pad[...] = jnp.zeros_like(pad)

    dmat = d_ref[...]
    lev = lev_ref[...]
    ones = jnp.ones((c, HG_DIM), BF16)
    tidx = lax.broadcasted_iota(jnp.int32, (c, HG_DIM), 0)
    for sub in range(nsub):
        rows = slice(sub * ntok, (sub + 1) * ntok)
        if ntok < c:
            for a, ref in enumerate((q_ref, k_ref, v_ref, lg_ref)):
                pad[sub, a, 0:ntok, :] = ref[rows, :]
            qr, kr, vr, lr = (pad.at[sub, a] for a in range(4))
        else:
            qr, kr, vr, lr = q_ref, k_ref, v_ref, lg_ref
        heads = [slice(h * HG_DIM, (h + 1) * HG_DIM) for h in range(HG_HEADS)]
        lgs = [_split_bf16(lr[:, hs]) for hs in heads]
        xs = [jnp.exp(_dot(dmat, hi) + _dot(dmat, lo)) for hi, lo in lgs]
        atts = []
        for hs, x in zip(heads, xs):
            q = qr[:, hs]
            k = kr[:, hs]
            att = jnp.where(lev == 0, _dot_nt(q.astype(BF16), k.astype(BF16)), 0.0)
            for l in range(1, nlev + 1):
                upper = ((tidx >> (l - 1)) & 1) == 1
                m = (jnp.where(upper, q, k) * x[(1 + l) * c:(2 + l) * c]).astype(BF16)
                att = jnp.where(lev == l, _dot_nt(m, m), att)
            atts.append(att.astype(BF16))
        for h, (hs, x, att) in enumerate(zip(heads, xs, atts)):
            qd = (qr[:, hs] * x[0:c]).astype(BF16)
            o = _dot(qd, s_scr[sub, h].astype(BF16)) + _dot(att, vr[:, hs].astype(BF16))
            o_ref[rows, hs] = o[0:ntok]
        for h, (hs, x, (hi, lo)) in enumerate(zip(heads, xs, lgs)):
            kd = (kr[:, hs] * x[c:2 * c]).astype(BF16)
            gl = _dot_tn(hi, ones) + _dot_tn(lo, ones)
            s_scr[sub, h] = jnp.exp(gl) * s_scr[sub, h] + _dot_tn(kd, vr[:, hs].astype(BF16))

    @pl.when(r == pl.num_programs(1) - 1)
    def _():
        sout_ref[...] = s_scr[...]


def _hgrn(proj, lg, s0, nseq, nchunks, c, row0, ntok=None):
    ntok = c if ntok is None else ntok
    dnp, levnp, nlev = _hgrn_consts(c)
    assert ntok == c or nchunks == 1
    nsub = 1 if ntok == c else SUBLANES // ntok
    step_rows = nsub * ntok
    assert row0 % step_rows == 0 and nseq % nsub == 0 and step_rows % SUBLANES == 0
    blk0 = row0 // step_rows
    tok_specs = [pl.BlockSpec((None, step_rows, D_MODEL),
                              functools.partial(lambda n, r, s: (s, blk0 + n * nchunks + r, 0), s=s))
                 for s in range(3)]
    lg_spec = pl.BlockSpec((step_rows, D_MODEL), lambda n, r: (blk0 + n * nchunks + r, 0))
    o_spec = pl.BlockSpec((step_rows, D_MODEL), lambda n, r: (n * nchunks + r, 0))
    o_shape = jax.ShapeDtypeStruct((nseq * nchunks * ntok, D_MODEL), F32)
    scratch = [] if ntok == c else [pltpu.VMEM((nsub, 4, c, D_MODEL), F32)]
    st_blk = (nsub, HG_HEADS, HG_DIM, HG_DIM)
    if s0.shape[0] == 1:
        s0_spec = pl.BlockSpec((1,) + st_blk[1:], lambda n, r: (0, 0, 0, 0))
    else:
        s0_spec = pl.BlockSpec(st_blk, lambda n, r: (n, 0, 0, 0))
    return pl.pallas_call(
        functools.partial(_hgrn_body, c=c, nlev=nlev, ntok=ntok, nsub=nsub),
        grid=(nseq // nsub, nchunks),
        in_specs=tok_specs + [
            lg_spec, s0_spec,
            pl.BlockSpec(dnp.shape, lambda n, r: (0, 0)),
            pl.BlockSpec(levnp.shape, lambda n, r: (0, 0)),
        ],
        out_specs=[o_spec, pl.BlockSpec(st_blk, lambda n, r: (n, 0, 0, 0))],
        out_shape=[o_shape, jax.ShapeDtypeStruct((nseq, HG_HEADS, HG_DIM, HG_DIM), F32)],
        scratch_shapes=[pltpu.VMEM(st_blk, F32)] + scratch,
        compiler_params=_params("arbitrary", "arbitrary"),
        name=f"hgrn_c{c}_t{ntok}",
    )(proj, proj, proj, lg, s0, jnp.asarray(dnp, BF16), jnp.asarray(levnp))


def _s5_compact(a_re, a_im, log_step, b_re, b_im, c_re, c_im, d_skip, nmax):
    hp = lax.Precision.HIGHEST
    lam_re = jnp.minimum(a_re.astype(F32), SSM_MIN_RE)
    lam_im = a_im.astype(F32)
    delta = jnp.exp(log_step.astype(F32))[:, None]
    n = jnp.arange(nmax + 1, dtype=F32)[:, None, None]
    mag = jnp.exp(lam_re * delta * n)
    ang = lam_im * delta * n
    pw_re = mag * jnp.cos(ang)
    pw_im = mag * jnp.sin(ang)
    x = pw_re[1] - 1.0
    y = pw_im[1]
    den = lam_re * lam_re + lam_im * lam_im
    cr = (x * lam_re + y * lam_im) / den
    ci = (y * lam_re - x * lam_im) / den
    bt_re = jnp.swapaxes(b_re.astype(F32), 1, 2)
    bt_im = jnp.swapaxes(b_im.astype(F32), 1, 2)
    bb_re = cr[:, None, :] * bt_re - ci[:, None, :] * bt_im
    bb_im = cr[:, None, :] * bt_im + ci[:, None, :] * bt_re
    w_re = pw_re[:nmax, :, None, :] * bb_re[None] - pw_im[:nmax, :, None, :] * bb_im[None]
    w_im = pw_re[:nmax, :, None, :] * bb_im[None] + pw_im[:nmax, :, None, :] * bb_re[None]
    cc_re = c_re.astype(F32)
    cc_im = c_im.astype(F32)
    kl = (jnp.einsum('gcp,ngdp->ngdc', cc_re, w_re, precision=hp)
          - jnp.einsum('gcp,ngdp->ngdc', cc_im, w_im, precision=hp))
    kl = kl.at[0].add(d_skip.astype(F32).reshape(SSM_G, SSM_GROUP)[:, :, None]
                      * jnp.eye(SSM_GROUP, dtype=F32)[None])
    ct_re = jnp.swapaxes(cc_re, 1, 2)
    ct_im = jnp.swapaxes(cc_im, 1, 2)
    p_re = pw_re[1:nmax + 1, :, :, None]
    p_im = pw_im[1:nmax + 1, :, :, None]
    wo_re = ct_re[None] * p_re - ct_im[None] * p_im
    wo_im = -(ct_re[None] * p_im + ct_im[None] * p_re)
    flat = lambda a: a.reshape(nmax, -1, a.shape[-1])
    return dict(kl=flat(kl), w_re=flat(w_re), w_im=flat(w_im), wo_re=flat(wo_re), wo_im=flat(wo_im),
                pw_re=pw_re, pw_im=pw_im)


def _s5_operators(cp, ntok):
    hp = lax.Precision.HIGHEST
    nb = SSM_G // S5_GB

    def block_diag(piece, rows_per_group):
        nn, rows, w = piece.shape
        rep = jnp.asarray(np.tile(np.eye(w, dtype=np.float32), (1, S5_GB)))
        row_g = (np.arange(rows) // rows_per_group) % S5_GB
        col_g = np.arange(S5_GB * w) // w
        mask = jnp.asarray((row_g[:, None] == col_g[None, :]).astype(np.float32))
        out = jnp.einsum('nrw,wl->nrl', piece, rep, precision=hp) * mask[None]
        return out.reshape(nn, nb, S5_GB * rows_per_group, S5_GB * w)

    bd_kl = block_diag(cp['kl'][:ntok], SSM_GROUP)
    zero = jnp.zeros_like(bd_kl[0])
    tz = jnp.stack([jnp.concatenate([bd_kl[t - s] if t >= s else zero for t in range(ntok)], axis=-1)
                    for s in range(ntok)], axis=1)
    state_in = lambda w: jnp.swapaxes(jnp.flip(block_diag(w[:ntok], SSM_GROUP), axis=0), 0, 1)
    state_out = lambda w: jnp.concatenate(list(block_diag(w[:ntok], SSM_P)), axis=-1)
    apow = jnp.stack([cp['pw_re'][ntok].reshape(nb, S5_GB * SSM_P),
                      cp['pw_im'][ntok].reshape(nb, S5_GB * SSM_P)], axis=1)
    pair = lambda a: a.reshape(nb, ntok // 2, 2 * LANES, a.shape[-1])
    tz = pair(tz)
    state_in = lambda w, f=state_in: pair(f(w))
    return dict(tz=tz.astype(BF16), wst_re=state_in(cp['w_re']), wst_im=state_in(cp['w_im']),
                wo_re=state_out(cp['wo_re']).astype(BF16), wo_im=state_out(cp['wo_im']).astype(BF16),
                apow=apow)


def _s5_body(u_ref, tz_ref, wre_ref, wim_ref, wore_ref, woim_ref, a_ref, x0re_ref, x0im_ref,
             y_ref, xre_ref, xim_ref, zre_scr, zim_scr, ub_scr,
             *, ntok, nchunks, nseq, npass):
    rows = nseq * nchunks
    nlb = (S5_GB * SSM_P) // LANES
    lane_blk = lambda lb: slice(lb * LANES, (lb + 1) * LANES)
    for sp in range(ntok // 2):
        us = jnp.concatenate([u_ref[pl.ds(2 * sp + d, rows, stride=ntok), :] for d in range(2)],
                             axis=1)
        uh, ul = _split_bf16(us)
        ub_scr[sp] = uh
        for w_ref, z_scr in ((wre_ref, zre_scr), (wim_ref, zim_scr)):
            wh, wl = _split_bf16(w_ref[sp])
            z = _dot(uh, wh)
            if npass >= 2:
                z = z + _dot(ul, wh)
            if npass >= 3:
                z = z + _dot(uh, wl)
            for lb in range(nlb):
                if sp == 0:
                    z_scr[lb] = z[:, lane_blk(lb)]
                else:
                    z_scr[lb] += z[:, lane_blk(lb)]
    a_re = a_ref[0:1, :]
    a_im = a_ref[1:2, :]

    def step(r, carry):
        x_re, x_im = carry
        at = pl.ds(r, nseq, stride=nchunks) if nchunks > 1 else pl.ds(0, nseq)
        z_re = jnp.concatenate([zre_scr[lb, at, :] for lb in range(nlb)], axis=1)
        z_im = jnp.concatenate([zim_scr[lb, at, :] for lb in range(nlb)], axis=1)
        for lb in range(nlb):
            zre_scr[lb, at, :] = x_re[:, lane_blk(lb)]
            zim_scr[lb, at, :] = x_im[:, lane_blk(lb)]
        return (a_re * x_re - a_im * x_im + z_re, a_re * x_im + a_im * x_re + z_im)

    x_re, x_im = lax.fori_loop(0, nchunks, step, (x0re_ref[0:nseq, :], x0im_ref[0:nseq, :]))
    xre_ref[...] = jnp.zeros_like(xre_ref)
    xim_ref[...] = jnp.zeros_like(xim_ref)
    xre_ref[0:nseq, :] = x_re
    xim_ref[0:nseq, :] = x_im
    for t0 in range(0, ntok, 2):
        cols = slice(t0 * LANES, (t0 + 2) * LANES)
        acc = jnp.zeros((rows, 2 * LANES), F32)
        for lb in range(0, nlb, 2):
            krows = slice(lb * LANES, (lb + 2) * LANES)
            for z_scr, wo_ref in ((zre_scr, wore_ref), (zim_scr, woim_ref)):
                xin = jnp.concatenate([z_scr[lb], z_scr[lb + 1]], axis=1).astype(BF16)
                acc = acc + _dot(xin, wo_ref[krows, cols])
        for sp in range(t0 // 2 + 1):
            acc = acc + _dot(ub_scr[sp], tz_ref[sp, :, cols])
        y_ref[pl.ds(t0, rows, stride=ntok), :] = acc[:, :LANES]
        y_ref[pl.ds(t0 + 1, rows, stride=ntok), :] = acc[:, LANES:]


def _s5(u, slot, ops, x0_re, x0_im, row0, nblk, nseq, nchunks, ntok, npass):
    nb = SSM_G // S5_GB
    rows = nseq * nchunks
    trows = rows * ntok
    assert row0 % trows == 0 and ntok % 2 == 0
    blk0 = row0 // trows
    nseq_pad = x0_re.shape[1]
    lanes = S5_GB * SSM_P
    if u.ndim == 3:
        u_spec = pl.BlockSpec((None, trows, LANES), lambda g, sb: (slot, blk0 + sb, g))
    else:
        u_spec = pl.BlockSpec((trows, LANES), lambda g, sb: (blk0 + sb, g))
    per_g = lambda shape: pl.BlockSpec((None,) + shape, lambda g, sb: (g,) + (0,) * len(shape))
    st_spec = pl.BlockSpec((None, nseq_pad, lanes), lambda g, sb: (sb, 0, g))
    st_shape = jax.ShapeDtypeStruct((nblk, nseq_pad, SSM_G * SSM_P), F32)
    return pl.pallas_call(
        functools.partial(_s5_body, ntok=ntok, nchunks=nchunks, nseq=nseq, npass=npass),
        grid=(nb, nblk),
        in_specs=[
            u_spec,
            per_g((ntok // 2, 2 * LANES, ntok * LANES)),
            per_g((ntok // 2, 2 * LANES, lanes)), per_g((ntok // 2, 2 * LANES, lanes)),
            per_g((lanes, ntok * LANES)), per_g((lanes, ntok * LANES)),
            per_g((2, lanes)),
            st_spec, st_spec,
        ],
        out_specs=[pl.BlockSpec((trows, LANES), lambda g, sb: (sb, g)), st_spec, st_spec],
        out_shape=[jax.ShapeDtypeStruct((nblk * trows, D_MODEL), F32), st_shape, st_shape],
        scratch_shapes=[pltpu.VMEM((lanes // LANES, rows, LANES), F32),
                        pltpu.VMEM((lanes // LANES, rows, LANES), F32),
                        pltpu.VMEM((ntok // 2, rows, 2 * LANES), BF16)],
        compiler_params=_params("arbitrary", "arbitrary"),
        name=f"s5_r{rows}_t{ntok}",
    )(u, ops['tz'], ops['wst_re'], ops['wst_im'], ops['wo_re'], ops['wo_im'], ops['apow'],
      x0_re, x0_im)


def _post_body(xa_ref, xb_ref, oa_ref, ob_ref, ya_ref, yb_ref, sg_ref, sa_ref, sb_ref, gnw_ref,
               wglu_ref, bglu_ref, wa_ref, wb_ref, wout_ref, out_ref, a_scr, *, n_first):
    first = pl.program_id(0) < n_first
    pick = lambda a_ref, b_ref: jnp.where(first, a_ref[...], b_ref[...])
    gnw = gnw_ref[...]
    o = pick(oa_ref, ob_ref)
    for h in range(HG_HEADS):
        hs = slice(h * HG_DIM, (h + 1) * HG_DIM)
        oh = o[:, hs]
        ms = jnp.mean(oh * oh, axis=-1, keepdims=True)
        a_scr[:, hs] = (oh * lax.rsqrt(ms + EPS) * gnw * sg_ref[:, hs]).astype(BF16)
    br_a = _dot(a_scr[...], wa_ref[...])
    y = _gelu_exact(pick(ya_ref, yb_ref))
    y = y * _sigmoid(_dot(y.astype(BF16), wglu_ref[...]) + bglu_ref[...])
    br_b = _dot(y.astype(BF16), wb_ref[...])
    mixed = sa_ref[...] * br_a + sb_ref[...] * br_b
    out_ref[...] = pick(xa_ref, xb_ref) + _dot(mixed.astype(BF16), wout_ref[...])


def _post(xa, xb, oa, ob, ya, yb, proj, g_norm_w, w_glu, b_glu, w_a, w_b, w_out, tm):
    t = xa.shape[0] + xb.shape[0]
    n_first = xa.shape[0] // tm
    assert xa.shape[0] % tm == 0 and xb.shape[0] % tm == 0 and proj.shape[1] == t
    spec_a, spec_b = _two_source_specs(tm, n_first)
    slot = lambda s: pl.BlockSpec((None, tm, D_MODEL), lambda i: (s, i, 0))
    full = lambda shape: pl.BlockSpec(shape, lambda i: (0,) * len(shape))
    sq = (D_MODEL, D_MODEL)
    return pl.pallas_call(
        functools.partial(_post_body, n_first=n_first),
        grid=(t // tm,),
        in_specs=[spec_a, spec_b, spec_a, spec_b, spec_a, spec_b, slot(3), slot(5), slot(6),
                  full((1, HG_DIM)), full(sq), full((1, D_MODEL)), full(sq), full(sq), full(sq)],
        out_specs=pl.BlockSpec((tm, D_MODEL), lambda i: (i, 0)),
        out_shape=jax.ShapeDtypeStruct((t, D_MODEL), F32),
        scratch_shapes=[pltpu.VMEM((tm, D_MODEL), BF16)],
        compiler_params=_params("arbitrary"),
        name="post",
    )(xa, xb, oa, ob, ya, yb, proj, proj, proj, g_norm_w.reshape(1, HG_DIM), w_glu.astype(BF16),
      b_glu.reshape(1, D_MODEL), w_a.astype(BF16), w_b.astype(BF16), w_out.astype(BF16))


def _top_rows(w, count):
    rows = []
    for _ in range(count):
        m = jnp.max(w, axis=0, keepdims=True)
        rows.append(m)
        w = jnp.where(w == m, -jnp.inf, w)
    return rows


_CAND_COUNTS = tuple(PEER_TOPK // (a + 1) for a in range(SUBLANES))


def _peer_body(x_ref, n2_ref, fn_ref, wq_ref, k1_ref, k2_ref, u_ref, vt_ref, ya_ref, yb_ref,
               hnt_scr, s_scr, p_scr, tau_scr, vs_scr, act0, act1, g0, g1, acc_scr,
               *, tm, eb, n_first):
    i = pl.program_id(0)
    j = pl.program_id(1)
    nblk = pl.num_programs(1) - 2
    nlt = tm // LANES
    nrow = eb // PEER_NKEYS
    assert nrow == SUBLANES

    @pl.when(j == 0)
    def _():
        act1[...] = jnp.zeros_like(act1)
        g1[...] = jnp.zeros_like(g1)
        acc_scr[...] = jnp.zeros_like(acc_scr)
        x = x_ref[...]
        ms = jnp.mean(x * x, axis=-1, keepdims=True)
        hn32 = x * lax.rsqrt(ms + EPS) * n2_ref[...]
        hnt_scr[...] = hn32.T.astype(BF16)
        hn = hn32.astype(BF16)
        for h in range(PEER_HEADS):
            c0 = 2 * h * PEER_HALF
            q2 = _dot(hn, wq_ref[:, c0:c0 + 2 * PEER_HALF]).astype(BF16)
            for half, k_ref in enumerate((k1_ref, k2_ref)):
                sc = _dot_nt(k_ref[h], q2[:, half * PEER_HALF:(half + 1) * PEER_HALF])
                for lt in range(nlt):
                    s_scr[2 * h + half, lt] = sc[:, lt * LANES:(lt + 1) * LANES]

        def top_body(idx, carry):
            hh = idx // nlt
            lt = idx % nlt
            vs_scr[hh, lt] = jnp.concatenate(_top_rows(s_scr[hh, lt], PEER_TOPK), axis=0)
            return carry

        lax.fori_loop(0, 2 * PEER_HEADS * nlt, top_body, 0)

        def gate_body(idx, carry):
            h = idx // nlt
            lt = idx % nlt
            v1 = vs_scr[2 * h, lt]
            v2 = vs_scr[2 * h + 1, lt]
            rank = lax.broadcasted_iota(jnp.int32, (PEER_TOPK, LANES), 0)
            cand = jnp.concatenate(
                [jnp.where(rank < cnt, v1[a:a + 1] + v2, -jnp.inf) for a, cnt in enumerate(_CAND_COUNTS)]
                + [v1[SUBLANES:] + v2[0:1]], axis=0)
            best = _top_rows(cand, PEER_TOPK + 1)
            c16 = best[PEER_TOPK - 1]
            top = v1[0:1] + v2[0:1]
            z = jnp.sum(jnp.where(cand >= c16, jnp.exp(cand - top), 0.0), axis=0, keepdims=True)
            tau = 0.5 * (c16 + best[PEER_TOPK])
            tau_scr[h, lt] = jnp.broadcast_to(tau, (SUBLANES, LANES))
            s1 = s_scr[2 * h, lt]
            s2 = s_scr[2 * h + 1, lt]
            p_scr[2 * h, lt] = jnp.where(s1 >= v1[PEER_TOPK - 1:], jnp.exp(s1 - v1[0:1]), 0.0) / z
            p_scr[2 * h + 1, lt] = jnp.where(s2 >= v2[PEER_TOPK - 1:], jnp.exp(s2 - v2[0:1]), 0.0)
            return carry

        lax.fori_loop(0, PEER_HEADS * nlt, gate_body, 0)

    e1_rows = pl.ds(pl.multiple_of(jnp.clip(j - 1, 0, nblk - 1) * nrow, nrow), nrow)

    def stages(act_new, act_cur, g_new, g_cur):
        def pre_activation(rows):
            act_new[rows, :] = _dot(u_ref[rows, :], hnt_scr[...])

        def accumulate(rows):
            acc_scr[rows, :] += _dot(vt_ref[rows, :], g_cur[...])

        def gated(lt, r0, kh):
            cols = slice(lt * LANES, (lt + 1) * LANES)
            keys = slice(kh * half_keys, (kh + 1) * half_keys)
            ws = [jnp.zeros((half_keys, LANES), F32) for _ in range(2)]
            for h in range(PEER_HEADS):
                s2 = s_scr[2 * h + 1, lt, keys, :]
                p2 = p_scr[2 * h + 1, lt, keys, :]
                s1grp = s_scr[2 * h, lt, e1_rows, :]
                p1grp = p_scr[2 * h, lt, e1_rows, :]
                tau = tau_scr[h, lt][0:1]
                for d in range(2):
                    theta = tau - s1grp[r0 + d:r0 + d + 1]
                    ws[d] = ws[d] + jnp.where(s2 >= theta, p2, 0.0) * p1grp[r0 + d:r0 + d + 1]
            for d in range(2):
                rows = slice((r0 + d) * PEER_NKEYS + kh * half_keys, (r0 + d) * PEER_NKEYS + (kh + 1) * half_keys)
                g_new[rows, cols] = (ws[d] * _gelu_exact(act_cur[rows, cols])).astype(BF16)

        chunk = 4 * LANES
        matmul_jobs = ([functools.partial(pre_activation, slice(m, m + chunk)) for m in range(0, eb, chunk)]
                       + [functools.partial(accumulate, slice(m, m + chunk)) for m in range(0, D_MODEL, chunk)])
        half_keys = PEER_NKEYS // 2
        gated_jobs = [functools.partial(gated, lt, r0, kh)
                      for lt in range(nlt) for r0 in range(0, nrow, 2) for kh in range(2)]
        per = -(-len(gated_jobs) // len(matmul_jobs))
        for k, job in enumerate(matmul_jobs):
            job()
            for gjob in gated_jobs[k * per:(k + 1) * per]:
                gjob()

    @pl.when(j % 2 == 0)
    def _():
        stages(act0, act1, g0, g1)

    @pl.when(j % 2 == 1)
    def _():
        stages(act1, act0, g1, g0)

    def finish(y_ref):
        xo = x_ref[...] + acc_scr[...].T
        ms = jnp.mean(xo * xo, axis=-1, keepdims=True)
        y_ref[...] = xo * lax.rsqrt(ms + EPS) * fn_ref[...]

    last = j == pl.num_programs(1) - 1

    @pl.when(last & (i < n_first))
    def _():
        finish(ya_ref)

    @pl.when(last & (i >= n_first))
    def _():
        finish(yb_ref)


def _peer(x, n_first_rows, norm2_w, final_norm_w, wq, k1, k2, u_tab, v_tab, tm, eb):
    t = x.shape[0]
    assert t % tm == 0 and n_first_rows % tm == 0
    n_first = n_first_rows // tm
    nexp = u_tab.shape[0]
    assert nexp == PEER_NKEYS * PEER_NKEYS and nexp % eb == 0
    nblk = nexp // eb
    qw = 2 * PEER_HEADS * PEER_HALF
    nlt = tm // LANES
    full = lambda shape: pl.BlockSpec(shape, lambda i, j: (0,) * len(shape))
    out_a, out_b = _two_source_specs(tm, n_first)
    return pl.pallas_call(
        functools.partial(_peer_body, tm=tm, eb=eb, n_first=n_first),
        grid=(t // tm, nblk + 2),
        in_specs=[
            pl.BlockSpec((tm, D_MODEL), lambda i, j: (i, 0)),
            full((1, D_MODEL)), full((1, D_MODEL)),
            full((D_MODEL, qw)),
            full((PEER_HEADS, PEER_NKEYS, PEER_HALF)), full((PEER_HEADS, PEER_NKEYS, PEER_HALF)),
            pl.BlockSpec((eb, D_MODEL), lambda i, j: (jnp.minimum(j, nblk - 1), 0)),
            pl.BlockSpec((None, D_MODEL, eb), lambda i, j: (jnp.clip(j - 2, 0, nblk - 1), 0, 0)),
        ],
        out_specs=[out_a, out_b],
        out_shape=[jax.ShapeDtypeStruct((n_first_rows, D_MODEL), F32),
                   jax.ShapeDtypeStruct((t - n_first_rows, D_MODEL), F32)],
        scratch_shapes=[
            pltpu.VMEM((D_MODEL, tm), BF16),
            pltpu.VMEM((2 * PEER_HEADS, nlt, PEER_NKEYS, LANES), F32),
            pltpu.VMEM((2 * PEER_HEADS, nlt, PEER_NKEYS, LANES), F32),
            pltpu.VMEM((PEER_HEADS, nlt, SUBLANES, LANES), F32),
            pltpu.VMEM((2 * PEER_HEADS, nlt, PEER_TOPK, LANES), F32),
            pltpu.VMEM((eb, tm), F32), pltpu.VMEM((eb, tm), F32),
            pltpu.VMEM((eb, tm), BF16), pltpu.VMEM((eb, tm), BF16),
            pltpu.VMEM((D_MODEL, tm), F32),
        ],
        compiler_params=_params("arbitrary", "arbitrary"),
        name="peer",
    )(x, norm2_w.reshape(1, D_MODEL), final_norm_w.reshape(1, D_MODEL), wq.astype(BF16),
      k1.astype(BF16), k2.astype(BF16), u_tab.astype(BF16),
      jnp.transpose(v_tab.astype(BF16).reshape(nblk, eb, D_MODEL), (0, 2, 1)))


def kernel(x_prompt, x_sample, state_hgrn, state_ssm_re, state_ssm_im, meta_tokens, lower_bounds,
           norm1_w, w_in, g_norm_w, ssm_a_re, ssm_a_im, ssm_log_step, ssm_b_re, ssm_b_im,
           ssm_c_re, ssm_c_im, ssm_d, w_glu, b_glu, w_branch_a, w_branch_b, w_out, norm2_w,
           peer_wq, peer_k1, peer_k2, peer_u, peer_v, final_norm_w):
    depth = w_in.shape[0]
    assert depth == 1, "single-layer step only"
    nb, seq, _ = x_prompt.shape
    ns, dseq, _ = x_sample.shape
    tp = nb * seq
    tsm = ns * dseq
    tm = 512
    hg_c = 128
    small_c = 16
    s5_tok = 8
    s5_seq = 4
    assert tp % tm == 0 and tsm % tm == 0 and seq % hg_c == 0 and seq % s5_tok == 0
    assert nb % s5_seq == 0 and dseq <= small_c and dseq % 2 == 0 and N_META % s5_tok == 0

    w_in_b = w_in[0].astype(BF16)
    xp = x_prompt.reshape(tp, D_MODEL).astype(F32)
    xs = x_sample.reshape(tsm, D_MODEL).astype(F32)
    proj, lg = _inproj(xp, xs, norm1_w[0], lower_bounds, w_in_b, tm)
    proj_m, lg_m = _inproj(meta_tokens.astype(F32), None, norm1_w[0], lower_bounds, w_in_b, N_META)

    zero_hg = jnp.zeros((1, HG_HEADS, HG_DIM, HG_DIM), F32)
    _, hg_meta = _hgrn(proj_m, lg_m, zero_hg, 1, 1, N_META, 0)
    o_p, hg_p = _hgrn(proj, lg, hg_meta, nb, seq // hg_c, hg_c, 0)
    o_s, hg_s = _hgrn(proj, lg, state_hgrn[0].astype(F32), ns, 1, small_c, tp, ntok=dseq)
    o_s = o_s.reshape(tsm, D_MODEL)

    ssm = (ssm_a_re[0], ssm_a_im[0], ssm_log_step[0], ssm_b_re[0], ssm_b_im[0], ssm_c_re[0],
           ssm_c_im[0], ssm_d[0])
    compact = _s5_compact(*ssm, max(s5_tok, dseq))
    ops_p = _s5_operators(compact, s5_tok)
    ops_s = _s5_operators(compact, dseq)
    nst = SSM_G * SSM_P
    u_meta = jnp.pad(proj_m[4], ((0, 7 * N_META), (0, 0)))
    zero_ss = jnp.zeros((1, SUBLANES, nst), F32)
    _, mre, mim = _s5(u_meta, 0, ops_p, zero_ss, zero_ss, 0, 1, SUBLANES, N_META // s5_tok, s5_tok, 3)
    nblk = nb // s5_seq
    x0re = jnp.broadcast_to(mre[:, 0:1], (nblk, SUBLANES, nst))
    x0im = jnp.broadcast_to(mim[:, 0:1], (nblk, SUBLANES, nst))
    y_p, pre, pim = _s5(proj, 4, ops_p, x0re, x0im, 0, nblk, s5_seq, seq // s5_tok, s5_tok, 1)
    y_s, sre, sim = _s5(proj, 4, ops_s, state_ssm_re[0].astype(F32).reshape(1, ns, nst),
                        state_ssm_im[0].astype(F32).reshape(1, ns, nst), tp, 1, ns, 1, dseq, 3)

    x1 = _post(xp, xs, o_p, o_s, y_p, y_s, proj, g_norm_w[0], w_glu[0], b_glu[0], w_branch_a[0],
               w_branch_b[0], w_out[0], tm)
    y_pr, y_sm = _peer(x1, tp, norm2_w[0], final_norm_w, peer_wq[0], peer_k1[0], peer_k2[0],
                       peer_u[0], peer_v[0], tm, SUBLANES * PEER_NKEYS)

    sd = state_hgrn.dtype
    st = lambda a: a[:, :s5_seq].reshape(nb, SSM_G, SSM_P)[None]
    return (y_pr.reshape(nb, seq, D_MODEL).astype(x_prompt.dtype),
            y_sm.reshape(ns, dseq, D_MODEL).astype(x_sample.dtype),
            hg_p[None].astype(sd),
            st(pre).astype(state_ssm_re.dtype),
            st(pim).astype(state_ssm_im.dtype),
            hg_s[None].astype(sd),
            sre.reshape(1, ns, SSM_G, SSM_P).astype(state_ssm_re.dtype),
            sim.reshape(1, ns, SSM_G, SSM_P).astype(state_ssm_im.dtype))
```

```python
import functools
import math

import numpy as np
import jax
import jax.numpy as jnp
from jax import lax
from jax.experimental import pallas as pl
from jax.experimental.pallas import tpu as pltpu

F32 = jnp.float32
BF16 = jnp.bfloat16

D_MODEL = 1024
N_META = 16
HG_HEADS = 8
HG_DIM = 128
SSM_G = 64
SSM_GROUP = 16
SSM_P = 64
SSM_MIN_RE = -1e-4
S5_GB = 8
PEER_HEADS = 8
PEER_NKEYS = 128
PEER_TOPK = 16
PEER_HALF = 128
EPS = 1e-6
LANES = 128
SUBLANES = 8
VMEM_LIMIT = 56 * 1024 * 1024


def _dot(a, b):
    return jnp.dot(a, b, preferred_element_type=F32)


def _dot_nt(a, b):
    return lax.dot_general(a, b, (((1,), (1,)), ((), ())), preferred_element_type=F32)


def _dot_tn(a, b):
    return lax.dot_general(a, b, (((0,), (0,)), ((), ())), preferred_element_type=F32)


def _split_bf16(x):
    hi = x.astype(BF16)
    lo = (x - hi.astype(F32)).astype(BF16)
    return hi, lo


def _sigmoid(x):
    return 1.0 / (1.0 + jnp.exp(-x))


def _gelu_exact(x):
    return 0.5 * x * (1.0 + lax.erf(x * (1.0 / math.sqrt(2.0))))


def _params(*sem):
    return pltpu.CompilerParams(dimension_semantics=sem, vmem_limit_bytes=VMEM_LIMIT)


def _two_source_specs(tm, n_first, ncols=D_MODEL):
    first = pl.BlockSpec((tm, ncols), lambda i, *_: (jnp.minimum(i, n_first - 1), 0))
    second = pl.BlockSpec((tm, ncols), lambda i, *_: (jnp.maximum(i - n_first, 0), 0))
    return first, second


N_SEG = 7


def _inproj_body(xa_ref, xb_ref, nw_ref, lbs_ref, w_ref, o_ref, lg_ref, h_scr, *, n_first):
    i = pl.program_id(0)
    j = pl.program_id(1)

    def normalise(x_ref):
        x = x_ref[...]
        ms = jnp.mean(x * x, axis=-1, keepdims=True)
        h_scr[...] = (x * lax.rsqrt(ms + EPS) * nw_ref[...]).astype(BF16)

    @pl.when((j == 0) & (i < n_first))
    def _():
        normalise(xa_ref)

    @pl.when((j == 0) & (i >= n_first))
    def _():
        normalise(xb_ref)

    p = _dot(h_scr[...], w_ref[...])

    @pl.when((j == 0) | (j == 3))
    def _():
        o_ref[...] = p * _sigmoid(p)

    @pl.when(j == 1)
    def _():
        lbs = lbs_ref[...]
        e = jnp.exp(lbs - jnp.max(lbs, axis=0, keepdims=True))
        lb = e[0:1] / jnp.sum(e, axis=0, keepdims=True)
        fg = lb + (1.0 - lb) * _sigmoid(p)
        o_ref[...] = 1.0 - fg
        lg_ref[...] = jnp.log(fg)

    @pl.when((j == 2) | (j == 4))
    def _():
        o_ref[...] = p

    @pl.when(j >= 5)
    def _():
        o_ref[...] = _sigmoid(p)


def _inproj(xa, xb, norm_w, lower_bounds, w_in_bf16, tm):
    if xb is None:
        xb = xa
        t = xa.shape[0]
        n_first = t // tm
    else:
        t = xa.shape[0] + xb.shape[0]
        n_first = xa.shape[0] // tm
        assert xa.shape[0] % tm == 0 and xb.shape[0] % tm == 0
    spec_a, spec_b = _two_source_specs(tm, n_first)
    return pl.pallas_call(
        functools.partial(_inproj_body, n_first=n_first),
        grid=(t // tm, N_SEG),
        in_specs=[
            spec_a, spec_b,
            pl.BlockSpec((1, D_MODEL), lambda i, j: (0, 0)),
            pl.BlockSpec(lower_bounds.shape, lambda i, j: (0, 0)),
            pl.BlockSpec((D_MODEL, D_MODEL), lambda i, j: (0, j)),
        ],
        out_specs=[
            pl.BlockSpec((None, tm, D_MODEL), lambda i, j: (j, i, 0)),
            pl.BlockSpec((tm, D_MODEL), lambda i, j: (i, 0)),
        ],
        out_shape=[
            jax.ShapeDtypeStruct((N_SEG, t, D_MODEL), F32),
            jax.ShapeDtypeStruct((t, D_MODEL), F32),
        ],
        scratch_shapes=[pltpu.VMEM((tm, D_MODEL), BF16)],
        compiler_params=_params("arbitrary", "arbitrary"),
        name="inproj",
    )(xa, xb, norm_w.reshape(1, D_MODEL), lower_bounds, w_in_bf16)


def _hgrn_consts(c):
    nlev = int(round(math.log2(c)))
    assert 1 << nlev == c
    d = np.zeros(((nlev + 2) * c, c), np.float32)
    for t in range(c):
        d[t, :t + 1] = 1.0
        d[c + t, t + 1:] = 1.0
        for l in range(1, nlev + 1):
            blk = 1 << l
            half = blk >> 1
            pos = t % blk
            m = t - pos + half
            row = (1 + l) * c + t
            if pos >= half:
                d[row, m:t + 1] = 1.0
            else:
                d[row, t + 1:m] = 1.0
    lev = np.full((c, c), -1, np.int32)
    for t in range(c):
        for s in range(t + 1):
            lev[t, s] = (t ^ s).bit_length()
    return d, lev, nlev


def _hgrn_body(q_ref, k_ref, v_ref, lg_ref, s0_ref, d_ref, lev_ref, o_ref, sout_ref, s_scr,
               *pad_scr, c, nlev, ntok, nsub):
    n = pl.program_id(0)
    r = pl.program_id(1)

    @pl.when(r == 0)
    def _():
        for sub in range(nsub):
            s_scr[sub] = s0_ref[sub if s0_ref.shape[0] == nsub else 0]

    if ntok < c:
        (pad,) = pad_scr

        @pl.when((n == 0) & (r == 0))
        def _():
            pad[...] = jnp.zeros_like(pad)

    dmat = d_ref[...]
    lev = lev_ref[...]
    ones = jnp.ones((c, HG_DIM), BF16)
    tidx = lax.broadcasted_iota(jnp.int32, (c, HG_DIM), 0)
    for sub in range(nsub):
        rows = slice(sub * ntok, (sub + 1) * ntok)
        if ntok < c:
            for a, ref in enumerate((q_ref, k_ref, v_ref, lg_ref)):
                pad[sub, a, 0:ntok, :] = ref[rows, :]
            qr, kr, vr, lr = (pad.at[sub, a] for a in range(4))
        else:
            qr, kr, vr, lr = q_ref, k_ref, v_ref, lg_ref
        heads = [slice(h * HG_DIM, (h + 1) * HG_DIM) for h in range(HG_HEADS)]
        lgs = [_split_bf16(lr[:, hs]) for hs in heads]
        xs = [jnp.exp(_dot(dmat, hi) + _dot(dmat, lo)) for hi, lo in lgs]
        atts = []
        for hs, x in zip(heads, xs):
            q = qr[:, hs]
            k = kr[:, hs]
            att = jnp.where(lev == 0, _dot_nt(q.astype(BF16), k.astype(BF16)), 0.0)
            for l in range(1, nlev + 1):
                upper = ((tidx >> (l - 1)) & 1) == 1
                m = (jnp.where(upper, q, k) * x[(1 + l) * c:(2 + l) * c]).astype(BF16)
                att = jnp.where(lev == l, _dot_nt(m, m), att)
            atts.append(att.astype(BF16))
        for h, (hs, x, att) in enumerate(zip(heads, xs, atts)):
            qd = (qr[:, hs] * x[0:c]).astype(BF16)
            o = _dot(qd, s_scr[sub, h].astype(BF16)) + _dot(att, vr[:, hs].astype(BF16))
            o_ref[rows, hs] = o[0:ntok]
        for h, (hs, x, (hi, lo)) in enumerate(zip(heads, xs, lgs)):
            kd = (kr[:, hs] * x[c:2 * c]).astype(BF16)
            gl = _dot_tn(hi, ones) + _dot_tn(lo, ones)
            s_scr[sub, h] = jnp.exp(gl) * s_scr[sub, h] + _dot_tn(kd, vr[:, hs].astype(BF16))

    @pl.when(r == pl.num_programs(1) - 1)
    def _():
        sout_ref[...] = s_scr[...]


def _hgrn(proj, lg, s0, nseq, nchunks, c, row0, ntok=None):
    ntok = c if ntok is None else ntok
    dnp, levnp, nlev = _hgrn_consts(c)
    assert ntok == c or nchunks == 1
    nsub = 1 if ntok == c else SUBLANES // ntok
    step_rows = nsub * ntok
    assert row0 % step_rows == 0 and nseq % nsub == 0 and step_rows % SUBLANES == 0
    blk0 = row0 // step_rows
    tok_specs = [pl.BlockSpec((None, step_rows, D_MODEL),
                              functools.partial(lambda n, r, s: (s, blk0 + n * nchunks + r, 0), s=s))
                 for s in range(3)]
    lg_spec = pl.BlockSpec((step_rows, D_MODEL), lambda n, r: (blk0 + n * nchunks + r, 0))
    o_spec = pl.BlockSpec((step_rows, D_MODEL), lambda n, r: (n * nchunks + r, 0))
    o_shape = jax.ShapeDtypeStruct((nseq * nchunks * ntok, D_MODEL), F32)
    scratch = [] if ntok == c else [pltpu.VMEM((nsub, 4, c, D_MODEL), F32)]
    st_blk = (nsub, HG_HEADS, HG_DIM, HG_DIM)
    if s0.shape[0] == 1:
        s0_spec = pl.BlockSpec((1,) + st_blk[1:], lambda n, r: (0, 0, 0, 0))
    else:
        s0_spec = pl.BlockSpec(st_blk, lambda n, r: (n, 0, 0, 0))
    return pl.pallas_call(
        functools.partial(_hgrn_body, c=c, nlev=nlev, ntok=ntok, nsub=nsub),
        grid=(nseq // nsub, nchunks),
        in_specs=tok_specs + [
            lg_spec, s0_spec,
            pl.BlockSpec(dnp.shape, lambda n, r: (0, 0)),
            pl.BlockSpec(levnp.shape, lambda n, r: (0, 0)),
        ],
        out_specs=[o_spec, pl.BlockSpec(st_blk, lambda n, r: (n, 0, 0, 0))],
        out_shape=[o_shape, jax.ShapeDtypeStruct((nseq, HG_HEADS, HG_DIM, HG_DIM), F32)],
        scratch_shapes=[pltpu.VMEM(st_blk, F32)] + scratch,
        compiler_params=_params("arbitrary", "arbitrary"),
        name=f"hgrn_c{c}_t{ntok}",
    )(proj, proj, proj, lg, s0, jnp.asarray(dnp, BF16), jnp.asarray(levnp))


def _s5_compact(a_re, a_im, log_step, b_re, b_im, c_re, c_im, d_skip, nmax):
    hp = lax.Precision.HIGHEST
    lam_re = jnp.minimum(a_re.astype(F32), SSM_MIN_RE)
    lam_im = a_im.astype(F32)
    delta = jnp.exp(log_step.astype(F32))[:, None]
    n = jnp.arange(nmax + 1, dtype=F32)[:, None, None]
    mag = jnp.exp(lam_re * delta * n)
    ang = lam_im * delta * n
    pw_re = mag * jnp.cos(ang)
    pw_im = mag * jnp.sin(ang)
    x = pw_re[1] - 1.0
    y = pw_im[1]
    den = lam_re * lam_re + lam_im * lam_im
    cr = (x * lam_re + y * lam_im) / den
    ci = (y * lam_re - x * lam_im) / den
    bt_re = jnp.swapaxes(b_re.astype(F32), 1, 2)
    bt_im = jnp.swapaxes(b_im.astype(F32), 1, 2)
    bb_re = cr[:, None, :] * bt_re - ci[:, None, :] * bt_im
    bb_im = cr[:, None, :] * bt_im + ci[:, None, :] * bt_re
    w_re = pw_re[:nmax, :, None, :] * bb_re[None] - pw_im[:nmax, :, None, :] * bb_im[None]
    w_im = pw_re[:nmax, :, None, :] * bb_im[None] + pw_im[:nmax, :, None, :] * bb_re[None]
    cc_re = c_re.astype(F32)
    cc_im = c_im.astype(F32)
    kl = (jnp.einsum('gcp,ngdp->ngdc', cc_re, w_re, precision=hp)
          - jnp.einsum('gcp,ngdp->ngdc', cc_im, w_im, precision=hp))
    kl = kl.at[0].add(d_skip.astype(F32).reshape(SSM_G, SSM_GROUP)[:, :, None]
                      * jnp.eye(SSM_GROUP, dtype=F32)[None])
    ct_re = jnp.swapaxes(cc_re, 1, 2)
    ct_im = jnp.swapaxes(cc_im, 1, 2)
    p_re = pw_re[1:nmax + 1, :, :, None]
    p_im = pw_im[1:nmax + 1, :, :, None]
    wo_re = ct_re[None] * p_re - ct_im[None] * p_im
    wo_im = -(ct_re[None] * p_im + ct_im[None] * p_re)
    flat = lambda a: a.reshape(nmax, -1, a.shape[-1])
    return dict(kl=flat(kl), w_re=flat(w_re), w_im=flat(w_im), wo_re=flat(wo_re), wo_im=flat(wo_im),
                pw_re=pw_re, pw_im=pw_im)


def _s5_operators(cp, ntok):
    hp = lax.Precision.HIGHEST
    nb = SSM_G // S5_GB

    def block_diag(piece, rows_per_group):
        nn, rows, w = piece.shape
        rep = jnp.asarray(np.tile(np.eye(w, dtype=np.float32), (1, S5_GB)))
        row_g = (np.arange(rows) // rows_per_group) % S5_GB
        col_g = np.arange(S5_GB * w) // w
        mask = jnp.asarray((row_g[:, None] == col_g[None, :]).astype(np.float32))
        out = jnp.einsum('nrw,wl->nrl', piece, rep, precision=hp) * mask[None]
        return out.reshape(nn, nb, S5_GB * rows_per_group, S5_GB * w)

    bd_kl = block_diag(cp['kl'][:ntok], SSM_GROUP)
    zero = jnp.zeros_like(bd_kl[0])
    tz = jnp.stack([jnp.concatenate([bd_kl[t - s] if t >= s else zero for t in range(ntok)], axis=-1)
                    for s in range(ntok)], axis=1)
    state_in = lambda w: jnp.swapaxes(jnp.flip(block_diag(w[:ntok], SSM_GROUP), axis=0), 0, 1)
    state_out = lambda w: jnp.concatenate(list(block_diag(w[:ntok], SSM_P)), axis=-1)
    apow = jnp.stack([cp['pw_re'][ntok].reshape(nb, S5_GB * SSM_P),
                      cp['pw_im'][ntok].reshape(nb, S5_GB * SSM_P)], axis=1)
    pair = lambda a: a.reshape(nb, ntok // 2, 2 * LANES, a.shape[-1])
    tz = pair(tz)
    state_in = lambda w, f=state_in: pair(f(w))
    return dict(tz=tz.astype(BF16), wst_re=state_in(cp['w_re']), wst_im=state_in(cp['w_im']),
                wo_re=state_out(cp['wo_re']).astype(BF16), wo_im=state_out(cp['wo_im']).astype(BF16),
                apow=apow)


def _s5_body(u_ref, tz_ref, wre_ref, wim_ref, wore_ref, woim_ref, a_ref, x0re_ref, x0im_ref,
             y_ref, xre_ref, xim_ref, zre_scr, zim_scr, ub_scr,
             *, ntok, nchunks, nseq, npass):
    rows = nseq * nchunks
    nlb = (S5_GB * SSM_P) // LANES
    lane_blk = lambda lb: slice(lb * LANES, (lb + 1) * LANES)
    for sp in range(ntok // 2):
        us = jnp.concatenate([u_ref[pl.ds(2 * sp + d, rows, stride=ntok), :] for d in range(2)],
                             axis=1)
        uh, ul = _split_bf16(us)
        ub_scr[sp] = uh
        for w_ref, z_scr in ((wre_ref, zre_scr), (wim_ref, zim_scr)):
            wh, wl = _split_bf16(w_ref[sp])
            z = _dot(uh, wh)
            if npass >= 2:
                z = z + _dot(ul, wh)
            if npass >= 3:
                z = z + _dot(uh, wl)
            for lb in range(nlb):
                if sp == 0:
                    z_scr[lb] = z[:, lane_blk(lb)]
                else:
                    z_scr[lb] += z[:, lane_blk(lb)]
    a_re = a_ref[0:1, :]
    a_im = a_ref[1:2, :]

    def step(r, carry):
        x_re, x_im = carry
        at = pl.ds(r, nseq, stride=nchunks) if nchunks > 1 else pl.ds(0, nseq)
        z_re = jnp.concatenate([zre_scr[lb, at, :] for lb in range(nlb)], axis=1)
        z_im = jnp.concatenate([zim_scr[lb, at, :] for lb in range(nlb)], axis=1)
        for lb in range(nlb):
            zre_scr[lb, at, :] = x_re[:, lane_blk(lb)]
            zim_scr[lb, at, :] = x_im[:, lane_blk(lb)]
        return (a_re * x_re - a_im * x_im + z_re, a_re * x_im + a_im * x_re + z_im)

    x_re, x_im = lax.fori_loop(0, nchunks, step, (x0re_ref[0:nseq, :], x0im_ref[0:nseq, :]))
    xre_ref[...] = jnp.zeros_like(xre_ref)
    xim_ref[...] = jnp.zeros_like(xim_ref)
    xre_ref[0:nseq, :] = x_re
    xim_ref[0:nseq, :] = x_im
    for t0 in range(0, ntok, 2):
        cols = slice(t0 * LANES, (t0 + 2) * LANES)
        acc = jnp.zeros((rows, 2 * LANES), F32)
        for lb in range(0, nlb, 2):
            krows = slice(lb * LANES, (lb + 2) * LANES)
            for z_scr, wo_ref in ((zre_scr, wore_ref), (zim_scr, woim_ref)):
                xin = jnp.concatenate([z_scr[lb], z_scr[lb + 1]], axis=1).astype(BF16)
                acc = acc + _dot(xin, wo_ref[krows, cols])
        for sp in range(t0 // 2 + 1):
            acc = acc + _dot(ub_scr[sp], tz_ref[sp, :, cols])
        y_ref[pl.ds(t0, rows, stride=ntok), :] = acc[:, :LANES]
        y_ref[pl.ds(t0 + 1, rows, stride=ntok), :] = acc[:, LANES:]


def _s5(u, slot, ops, x0_re, x0_im, row0, nblk, nseq, nchunks, ntok, npass):
    nb = SSM_G // S5_GB
    rows = nseq * nchunks
    trows = rows * ntok
    assert row0 % trows == 0 and ntok % 2 == 0
    blk0 = row0 // trows
    nseq_pad = x0_re.shape[1]
    lanes = S5_GB * SSM_P
    if u.ndim == 3:
        u_spec = pl.BlockSpec((None, trows, LANES), lambda g, sb: (slot, blk0 + sb, g))
    else:
        u_spec = pl.BlockSpec((trows, LANES), lambda g, sb: (blk0 + sb, g))
    per_g = lambda shape: pl.BlockSpec((None,) + shape, lambda g, sb: (g,) + (0,) * len(shape))
    st_spec = pl.BlockSpec((None, nseq_pad, lanes), lambda g, sb: (sb, 0, g))
    st_shape = jax.ShapeDtypeStruct((nblk, nseq_pad, SSM_G * SSM_P), F32)
    return pl.pallas_call(
        functools.partial(_s5_body, ntok=ntok, nchunks=nchunks, nseq=nseq, npass=npass),
        grid=(nb, nblk),
        in_specs=[
            u_spec,
            per_g((ntok // 2, 2 * LANES, ntok * LANES)),
            per_g((ntok // 2, 2 * LANES, lanes)), per_g((ntok // 2, 2 * LANES, lanes)),
            per_g((lanes, ntok * LANES)), per_g((lanes, ntok * LANES)),
            per_g((2, lanes)),
            st_spec, st_spec,
        ],
        out_specs=[pl.BlockSpec((trows, LANES), lambda g, sb: (sb, g)), st_spec, st_spec],
        out_shape=[jax.ShapeDtypeStruct((nblk * trows, D_MODEL), F32), st_shape, st_shape],
        scratch_shapes=[pltpu.VMEM((lanes // LANES, rows, LANES), F32),
                        pltpu.VMEM((lanes // LANES, rows, LANES), F32),
                        pltpu.VMEM((ntok // 2, rows, 2 * LANES), BF16)],
        compiler_params=_params("arbitrary", "arbitrary"),
        name=f"s5_r{rows}_t{ntok}",
    )(u, ops['tz'], ops['wst_re'], ops['wst_im'], ops['wo_re'], ops['wo_im'], ops['apow'],
      x0_re, x0_im)


def _post_body(xa_ref, xb_ref, oa_ref, ob_ref, ya_ref, yb_ref, sg_ref, sa_ref, sb_ref, gnw_ref,
               wglu_ref, bglu_ref, wa_ref, wb_ref, wout_ref, out_ref, a_scr, *, n_first):
    first = pl.program_id(0) < n_first
    pick = lambda a_ref, b_ref: jnp.where(first, a_ref[...], b_ref[...])
    gnw = gnw_ref[...]
    o = pick(oa_ref, ob_ref)
    for h in range(HG_HEADS):
        hs = slice(h * HG_DIM, (h + 1) * HG_DIM)
        oh = o[:, hs]
        ms = jnp.mean(oh * oh, axis=-1, keepdims=True)
        a_scr[:, hs] = (oh * lax.rsqrt(ms + EPS) * gnw * sg_ref[:, hs]).astype(BF16)
    br_a = _dot(a_scr[...], wa_ref[...])
    y = _gelu_exact(pick(ya_ref, yb_ref))
    y = y * _sigmoid(_dot(y.astype(BF16), wglu_ref[...]) + bglu_ref[...])
    br_b = _dot(y.astype(BF16), wb_ref[...])
    mixed = sa_ref[...] * br_a + sb_ref[...] * br_b
    out_ref[...] = pick(xa_ref, xb_ref) + _dot(mixed.astype(BF16), wout_ref[...])


def _post(xa, xb, oa, ob, ya, yb, proj, g_norm_w, w_glu, b_glu, w_a, w_b, w_out, tm):
    t = xa.shape[0] + xb.shape[0]
    n_first = xa.shape[0] // tm
    assert xa.shape[0] % tm == 0 and xb.shape[0] % tm == 0 and proj.shape[1] == t
    spec_a, spec_b = _two_source_specs(tm, n_first)
    slot = lambda s: pl.BlockSpec((None, tm, D_MODEL), lambda i: (s, i, 0))
    full = lambda shape: pl.BlockSpec(shape, lambda i: (0,) * len(shape))
    sq = (D_MODEL, D_MODEL)
    return pl.pallas_call(
        functools.partial(_post_body, n_first=n_first),
        grid=(t // tm,),
        in_specs=[spec_a, spec_b, spec_a, spec_b, spec_a, spec_b, slot(3), slot(5), slot(6),
                  full((1, HG_DIM)), full(sq), full((1, D_MODEL)), full(sq), full(sq), full(sq)],
        out_specs=pl.BlockSpec((tm, D_MODEL), lambda i: (i, 0)),
        out_shape=jax.ShapeDtypeStruct((t, D_MODEL), F32),
        scratch_shapes=[pltpu.VMEM((tm, D_MODEL), BF16)],
        compiler_params=_params("arbitrary"),
        name="post",
    )(xa, xb, oa, ob, ya, yb, proj, proj, proj, g_norm_w.reshape(1, HG_DIM), w_glu.astype(BF16),
      b_glu.reshape(1, D_MODEL), w_a.astype(BF16), w_b.astype(BF16), w_out.astype(BF16))


def _sort_network(n):
    pairs = []
    p = 1
    while p < n:
        k = p
        while k >= 1:
            for j in range(k % p, n - k, 2 * k):
                for i in range(min(k, n - j - k)):
                    if (i + j) // (2 * p) == (i + j + k) // (2 * p):
                        pairs.append((i + j, i + j + k))
            k //= 2
        p *= 2
    return pairs


def _exchange(vs, i, j):
    vs[i], vs[j] = jnp.maximum(vs[i], vs[j]), jnp.minimum(vs[i], vs[j])


def _merge_top(a, b):
    n = len(a)
    vs = [jnp.maximum(a[i], b[n - 1 - i]) for i in range(n)]
    d = n // 2
    while d >= 1:
        for i in range(n):
            if i & d == 0:
                _exchange(vs, i, i + d)
        d //= 2
    return vs


def _merge_sublanes(vs):
    shift = SUBLANES // 2
    while shift >= 1:
        vs = _merge_top(vs, [pltpu.roll(v, shift, axis=0) for v in vs])
        shift //= 2
    return vs


def _top16_of_keys(s):
    vs = [s[v * SUBLANES:(v + 1) * SUBLANES] for v in range(PEER_NKEYS // SUBLANES)]
    for i, j in _sort_network(len(vs)):
        _exchange(vs, i, j)
    vs = _merge_sublanes(vs)
    return jnp.concatenate([v[0:1] for v in vs], axis=0)


_CAND_COUNTS = tuple(PEER_TOPK // (a + 1) for a in range(PEER_TOPK))


def _candidate_thresholds(v1, v2):
    sub = lax.broadcasted_iota(jnp.int32, (SUBLANES, LANES), 0)
    halves = []
    for half in range(PEER_TOPK // SUBLANES):
        v2h = v2[half * SUBLANES:(half + 1) * SUBLANES]
        halves.append([jnp.where(sub + half * SUBLANES < cnt, v1[a:a + 1] + v2h, -jnp.inf)
                       for a, cnt in enumerate(_CAND_COUNTS)])
    top = _merge_sublanes(_merge_top(halves[0], halves[1]))
    c16 = top[PEER_TOPK - 1][0:1]
    cands = halves[0] + halves[1]
    below = functools.reduce(jnp.maximum, [jnp.where(c < c16, c, -jnp.inf) for c in cands])
    c17 = jnp.max(below, axis=0, keepdims=True)
    return cands, c16, c17


def _peer_body(x_ref, n2_ref, fn_ref, wq_ref, k1_ref, k2_ref, u_ref, vt_ref, ya_ref, yb_ref,
               hnt_scr, s_scr, p_scr, tau_scr, vs_scr, act0, act1, g0, g1, acc_scr,
               *, tm, eb, n_first):
    i = pl.program_id(0)
    j = pl.program_id(1)
    nblk = pl.num_programs(1) - 2
    nlt = tm // LANES
    nrow = eb // PEER_NKEYS
    assert nrow == SUBLANES

    @pl.when(j == 0)
    def _():
        act1[...] = jnp.zeros_like(act1)
        g1[...] = jnp.zeros_like(g1)
        acc_scr[...] = jnp.zeros_like(acc_scr)
        x = x_ref[...]
        ms = jnp.mean(x * x, axis=-1, keepdims=True)
        hn32 = x * lax.rsqrt(ms + EPS) * n2_ref[...]
        hnt_scr[...] = hn32.T.astype(BF16)
        qall = _dot(hn32.astype(BF16), wq_ref[...]).astype(BF16)
        for h in range(PEER_HEADS):
            for half, k_ref in enumerate((k1_ref, k2_ref)):
                c0 = (2 * h + half) * PEER_HALF
                sc = _dot_nt(k_ref[h], qall[:, c0:c0 + PEER_HALF])
                for lt in range(nlt):
                    s_scr[2 * h + half, lt] = sc[:, lt * LANES:(lt + 1) * LANES]

        def top_body(idx, carry):
            hh = idx // nlt
            lt = idx % nlt
            vs_scr[hh, lt] = _top16_of_keys(s_scr[hh, lt])
            return carry

        lax.fori_loop(0, 2 * PEER_HEADS * nlt, top_body, 0)

        def gate_body(idx, carry):
            h = idx // nlt
            lt = idx % nlt
            v1 = vs_scr[2 * h, lt]
            v2 = vs_scr[2 * h + 1, lt]
            cands, c16, c17 = _candidate_thresholds(v1, v2)
            top = v1[0:1] + v2[0:1]
            z = functools.reduce(jnp.add, [jnp.where(c >= c16, jnp.exp(c - top), 0.0) for c in cands])
            z = jnp.sum(z, axis=0, keepdims=True)
            tau_scr[h, lt] = jnp.broadcast_to(0.5 * (c16 + c17), (SUBLANES, LANES))
            s1 = s_scr[2 * h, lt]
            s2 = s_scr[2 * h + 1, lt]
            p_scr[2 * h, lt] = jnp.where(s1 >= v1[PEER_TOPK - 1:], jnp.exp(s1 - v1[0:1]), 0.0) / z
            p_scr[2 * h + 1, lt] = jnp.where(s2 >= v2[PEER_TOPK - 1:], jnp.exp(s2 - v2[0:1]), 0.0)
            return carry

        lax.fori_loop(0, PEER_HEADS * nlt, gate_body, 0)

    e1_rows = pl.ds(pl.multiple_of(jnp.clip(j - 1, 0, nblk - 1) * nrow, nrow), nrow)
    half_keys = PEER_NKEYS // 2

    def stages(act_new, act_cur, g_new, g_cur):
        def pre_activation(rows):
            act_new[rows, :] = _dot(u_ref[rows, :], hnt_scr[...])

        def accumulate(rows):
            acc_scr[rows, :] += _dot(vt_ref[rows, :], g_cur[...])

        def gated(lt, r0, kh):
            cols = slice(lt * LANES, (lt + 1) * LANES)
            keys = slice(kh * half_keys, (kh + 1) * half_keys)
            ws = [jnp.zeros((half_keys, LANES), F32) for _ in range(2)]
            for h in range(PEER_HEADS):
                s2 = s_scr[2 * h + 1, lt, keys, :]
                p2 = p_scr[2 * h + 1, lt, keys, :]
                s1grp = s_scr[2 * h, lt, e1_rows, :]
                p1grp = p_scr[2 * h, lt, e1_rows, :]
                tau = tau_scr[h, lt][0:1]
                for d in range(2):
                    theta = tau - s1grp[r0 + d:r0 + d + 1]
                    ws[d] = ws[d] + jnp.where(s2 >= theta, p2, 0.0) * p1grp[r0 + d:r0 + d + 1]
            for d in range(2):
                rows = slice((r0 + d) * PEER_NKEYS + kh * half_keys, (r0 + d) * PEER_NKEYS + (kh + 1) * half_keys)
                g_new[rows, cols] = (ws[d] * _gelu_exact(act_cur[rows, cols])).astype(BF16)

        chunk = 4 * LANES
        matmul_jobs = ([functools.partial(pre_activation, slice(m, m + chunk)) for m in range(0, eb, chunk)]
                       + [functools.partial(accumulate, slice(m, m + chunk)) for m in range(0, D_MODEL, chunk)])
        gated_jobs = [functools.partial(gated, lt, r0, kh)
                      for lt in range(nlt) for r0 in range(0, nrow, 2) for kh in range(2)]
        per = -(-len(gated_jobs) // len(matmul_jobs))
        for k, job in enumerate(matmul_jobs):
            job()
            for gjob in gated_jobs[k * per:(k + 1) * per]:
                gjob()

    @pl.when(j % 2 == 0)
    def _():
        stages(act0, act1, g0, g1)

    @pl.when(j % 2 == 1)
    def _():
        stages(act1, act0, g1, g0)

    def finish(y_ref):
        xo = x_ref[...] + acc_scr[...].T
        ms = jnp.mean(xo * xo, axis=-1, keepdims=True)
        y_ref[...] = xo * lax.rsqrt(ms + EPS) * fn_ref[...]

    last = j == pl.num_programs(1) - 1

    @pl.when(last & (i < n_first))
    def _():
        finish(ya_ref)

    @pl.when(last & (i >= n_first))
    def _():
        finish(yb_ref)


def _peer(x, n_first_rows, norm2_w, final_norm_w, wq, k1, k2, u_tab, v_tab, tm, eb):
    t = x.shape[0]
    assert t % tm == 0 and n_first_rows % tm == 0
    n_first = n_first_rows // tm
    nexp = u_tab.shape[0]
    assert nexp == PEER_NKEYS * PEER_NKEYS and nexp % eb == 0
    nblk = nexp // eb
    qw = 2 * PEER_HEADS * PEER_HALF
    nlt = tm // LANES
    full = lambda shape: pl.BlockSpec(shape, lambda i, j: (0,) * len(shape))
    out_a, out_b = _two_source_specs(tm, n_first)
    return pl.pallas_call(
        functools.partial(_peer_body, tm=tm, eb=eb, n_first=n_first),
        grid=(t // tm, nblk + 2),
        in_specs=[
            pl.BlockSpec((tm, D_MODEL), lambda i, j: (i, 0)),
            full((1, D_MODEL)), full((1, D_MODEL)),
            full((D_MODEL, qw)),
            full((PEER_HEADS, PEER_NKEYS, PEER_HALF)), full((PEER_HEADS, PEER_NKEYS, PEER_HALF)),
            pl.BlockSpec((eb, D_MODEL), lambda i, j: (jnp.minimum(j, nblk - 1), 0)),
            pl.BlockSpec((None, D_MODEL, eb), lambda i, j: (jnp.clip(j - 2, 0, nblk - 1), 0, 0)),
        ],
        out_specs=[out_a, out_b],
        out_shape=[jax.ShapeDtypeStruct((n_first_rows, D_MODEL), F32),
                   jax.ShapeDtypeStruct((t - n_first_rows, D_MODEL), F32)],
        scratch_shapes=[
            pltpu.VMEM((D_MODEL, tm), BF16),
            pltpu.VMEM((2 * PEER_HEADS, nlt, PEER_NKEYS, LANES), F32),
            pltpu.VMEM((2 * PEER_HEADS, nlt, PEER_NKEYS, LANES), F32),
            pltpu.VMEM((PEER_HEADS, nlt, SUBLANES, LANES), F32),
            pltpu.VMEM((2 * PEER_HEADS, nlt, PEER_TOPK, LANES), F32),
            pltpu.VMEM((eb, tm), F32), pltpu.VMEM((eb, tm), F32),
            pltpu.VMEM((eb, tm), BF16), pltpu.VMEM((eb, tm), BF16),
            pltpu.VMEM((D_MODEL, tm), F32),
        ],
        compiler_params=_params("arbitrary", "arbitrary"),
        name="peer",
    )(x, norm2_w.reshape(1, D_MODEL), final_norm_w.reshape(1, D_MODEL), wq.astype(BF16),
      k1.astype(BF16), k2.astype(BF16), u_tab.astype(BF16),
      jnp.transpose(v_tab.astype(BF16).reshape(nblk, eb, D_MODEL), (0, 2, 1)))


def kernel(x_prompt, x_sample, state_hgrn, state_ssm_re, state_ssm_im, meta_tokens, lower_bounds,
           norm1_w, w_in, g_norm_w, ssm_a_re, ssm_a_im, ssm_log_step, ssm_b_re, ssm_b_im,
           ssm_c_re, ssm_c_im, ssm_d, w_glu, b_glu, w_branch_a, w_branch_b, w_out, norm2_w,
           peer_wq, peer_k1, peer_k2, peer_u, peer_v, final_norm_w):
    depth = w_in.shape[0]
    assert depth == 1, "single-layer step only"
    nb, seq, _ = x_prompt.shape
    ns, dseq, _ = x_sample.shape
    tp = nb * seq
    tsm = ns * dseq
    tm = 512
    hg_c = 128
    small_c = 16
    s5_tok = 8
    s5_seq = 4
    assert tp % tm == 0 and tsm % tm == 0 and seq % hg_c == 0 and seq % s5_tok == 0
    assert nb % s5_seq == 0 and dseq <= small_c and dseq % 2 == 0 and N_META % s5_tok == 0

    w_in_b = w_in[0].astype(BF16)
    xp = x_prompt.reshape(tp, D_MODEL).astype(F32)
    xs = x_sample.reshape(tsm, D_MODEL).astype(F32)
    proj, lg = _inproj(xp, xs, norm1_w[0], lower_bounds, w_in_b, tm)
    proj_m, lg_m = _inproj(meta_tokens.astype(F32), None, norm1_w[0], lower_bounds, w_in_b, N_META)

    zero_hg = jnp.zeros((1, HG_HEADS, HG_DIM, HG_DIM), F32)
    _, hg_meta = _hgrn(proj_m, lg_m, zero_hg, 1, 1, N_META, 0)
    o_p, hg_p = _hgrn(proj, lg, hg_meta, nb, seq // hg_c, hg_c, 0)
    o_s, hg_s = _hgrn(proj, lg, state_hgrn[0].astype(F32), ns, 1, small_c, tp, ntok=dseq)
    o_s = o_s.reshape(tsm, D_MODEL)

    ssm = (ssm_a_re[0], ssm_a_im[0], ssm_log_step[0], ssm_b_re[0], ssm_b_im[0], ssm_c_re[0],
           ssm_c_im[0], ssm_d[0])
    compact = _s5_compact(*ssm, max(s5_tok, dseq))
    ops_p = _s5_operators(compact, s5_tok)
    ops_s = _s5_operators(compact, dseq)
    nst = SSM_G * SSM_P
    u_meta = jnp.pad(proj_m[4], ((0, 7 * N_META), (0, 0)))
    zero_ss = jnp.zeros((1, SUBLANES, nst), F32)
    _, mre, mim = _s5(u_meta, 0, ops_p, zero_ss, zero_ss, 0, 1, SUBLANES, N_META // s5_tok, s5_tok, 3)
    nblk = nb // s5_seq
    x0re = jnp.broadcast_to(mre[:, 0:1], (nblk, SUBLANES, nst))
    x0im = jnp.broadcast_to(mim[:, 0:1], (nblk, SUBLANES, nst))
    y_p, pre, pim = _s5(proj, 4, ops_p, x0re, x0im, 0, nblk, s5_seq, seq // s5_tok, s5_tok, 1)
    y_s, sre, sim = _s5(proj, 4, ops_s, state_ssm_re[0].astype(F32).reshape(1, ns, nst),
                        state_ssm_im[0].astype(F32).reshape(1, ns, nst), tp, 1, ns, 1, dseq, 3)

    x1 = _post(xp, xs, o_p, o_s, y_p, y_s, proj, g_norm_w[0], w_glu[0], b_glu[0], w_branch_a[0],
               w_branch_b[0], w_out[0], tm)
    y_pr, y_sm = _peer(x1, tp, norm2_w[0], final_norm_w, peer_wq[0], peer_k1[0], peer_k2[0],
                       peer_u[0], peer_v[0], tm, SUBLANES * PEER_NKEYS)

    sd = state_hgrn.dtype
    st = lambda a: a[:, :s5_seq].reshape(nb, SSM_G, SSM_P)[None]
    return (y_pr.reshape(nb, seq, D_MODEL).astype(x_prompt.dtype),
            y_sm.reshape(ns, dseq, D_MODEL).astype(x_sample.dtype),
            hg_p[None].astype(sd),
            st(pre).astype(state_ssm_re.dtype),
            st(pim).astype(state_ssm_im.dtype),
            hg_s[None].astype(sd),
            sre.reshape(1, ns, SSM_G, SSM_P).astype(state_ssm_re.dtype),
            sim.reshape(1, ns, SSM_G, SSM_P).astype(state_ssm_im.dtype))
```

```python
import functools
import math

import numpy as np
import jax
import jax.numpy as jnp
from jax import lax
from jax.experimental import pallas as pl
from jax.experimental.pallas import tpu as pltpu

F32 = jnp.float32
BF16 = jnp.bfloat16

D_MODEL = 1024
N_META = 16
HG_HEADS = 8
HG_DIM = 128
SSM_G = 64
SSM_GROUP = 16
SSM_P = 64
SSM_MIN_RE = -1e-4
S5_GB = 8
PEER_HEADS = 8
PEER_NKEYS = 128
PEER_TOPK = 16
PEER_HALF = 128
EPS = 1e-6
LANES = 128
SUBLANES = 8
VMEM_LIMIT = 56 * 1024 * 1024


def _dot(a, b):
    return jnp.dot(a, b, preferred_element_type=F32)


def _dot_nt(a, b):
    return lax.dot_general(a, b, (((1,), (1,)), ((), ())), preferred_element_type=F32)


def _dot_tn(a, b):
    return lax.dot_general(a, b, (((0,), (0,)), ((), ())), preferred_element_type=F32)


def _split_bf16(x):
    hi = x.astype(BF16)
    lo = (x - hi.astype(F32)).astype(BF16)
    return hi, lo


def _sigmoid(x):
    return 1.0 / (1.0 + jnp.exp(-x))


def _gelu_exact(x):
    return 0.5 * x * (1.0 + lax.erf(x * (1.0 / math.sqrt(2.0))))


def _params(*sem):
    return pltpu.CompilerParams(dimension_semantics=sem, vmem_limit_bytes=VMEM_LIMIT)


def _two_source_specs(tm, n_first, ncols=D_MODEL):
    first = pl.BlockSpec((tm, ncols), lambda i, *_: (jnp.minimum(i, n_first - 1), 0))
    second = pl.BlockSpec((tm, ncols), lambda i, *_: (jnp.maximum(i - n_first, 0), 0))
    return first, second


N_SEG = 7


def _inproj_body(xa_ref, xb_ref, nw_ref, lbs_ref, w_ref, o_ref, lg_ref, h_scr, *, n_first):
    i = pl.program_id(0)
    j = pl.program_id(1)

    def normalise(x_ref):
        x = x_ref[...]
        ms = jnp.mean(x * x, axis=-1, keepdims=True)
        h_scr[...] = (x * lax.rsqrt(ms + EPS) * nw_ref[...]).astype(BF16)

    @pl.when((j == 0) & (i < n_first))
    def _():
        normalise(xa_ref)

    @pl.when((j == 0) & (i >= n_first))
    def _():
        normalise(xb_ref)

    p = _dot(h_scr[...], w_ref[...])

    @pl.when((j == 0) | (j == 3))
    def _():
        o_ref[...] = p * _sigmoid(p)

    @pl.when(j == 1)
    def _():
        lbs = lbs_ref[...]
        e = jnp.exp(lbs - jnp.max(lbs, axis=0, keepdims=True))
        lb = e[0:1] / jnp.sum(e, axis=0, keepdims=True)
        fg = lb + (1.0 - lb) * _sigmoid(p)
        o_ref[...] = 1.0 - fg
        lg_ref[...] = jnp.log(fg)

    @pl.when((j == 2) | (j == 4))
    def _():
        o_ref[...] = p

    @pl.when(j >= 5)
    def _():
        o_ref[...] = _sigmoid(p)


def _inproj(xa, xb, norm_w, lower_bounds, w_in_bf16, tm):
    if xb is None:
        xb = xa
        t = xa.shape[0]
        n_first = t // tm
    else:
        t = xa.shape[0] + xb.shape[0]
        n_first = xa.shape[0] // tm
        assert xa.shape[0] % tm == 0 and xb.shape[0] % tm == 0
    spec_a, spec_b = _two_source_specs(tm, n_first)
    return pl.pallas_call(
        functools.partial(_inproj_body, n_first=n_first),
        grid=(t // tm, N_SEG),
        in_specs=[
            spec_a, spec_b,
            pl.BlockSpec((1, D_MODEL), lambda i, j: (0, 0)),
            pl.BlockSpec(lower_bounds.shape, lambda i, j: (0, 0)),
            pl.BlockSpec((D_MODEL, D_MODEL), lambda i, j: (0, j)),
        ],
        out_specs=[
            pl.BlockSpec((None, tm, D_MODEL), lambda i, j: (j, i, 0)),
            pl.BlockSpec((tm, D_MODEL), lambda i, j: (i, 0)),
        ],
        out_shape=[
            jax.ShapeDtypeStruct((N_SEG, t, D_MODEL), F32),
            jax.ShapeDtypeStruct((t, D_MODEL), F32),
        ],
        scratch_shapes=[pltpu.VMEM((tm, D_MODEL), BF16)],
        compiler_params=_params("arbitrary", "arbitrary"),
        name="inproj",
    )(xa, xb, norm_w.reshape(1, D_MODEL), lower_bounds, w_in_bf16)


def _hgrn_consts(c):
    nlev = int(round(math.log2(c)))
    assert 1 << nlev == c and c % SUBLANES == 0
    tri = np.tril(np.ones((c, c), np.float32))
    lev = np.full((c, c), -1, np.int32)
    for t in range(c):
        for s in range(t + 1):
            lev[t, s] = (t ^ s).bit_length()
    return tri, lev, nlev


def _decay_factors(lg, tri, c, nlev):
    hi, lo = _split_bf16(lg)
    b = _dot(tri, hi) + _dot(tri, lo)
    row = lambda r: jnp.broadcast_to(b[r:r + 1], (SUBLANES, HG_DIM))
    sub = lax.broadcasted_iota(jnp.int32, (SUBLANES, HG_DIM), 0)
    tidx = lax.broadcasted_iota(jnp.int32, (c, HG_DIM), 0)
    out = [jnp.exp(b), jnp.exp(b[c - 1:c] - b)]
    for l in range(1, nlev + 1):
        blk = 1 << l
        half = blk >> 1
        upper = ((tidx >> (l - 1)) & 1) == 1
        if l == 1:
            e = jnp.where(upper, lg, 0.0)
        else:
            groups = []
            for t0 in range(0, c, SUBLANES):
                if blk >= SUBLANES:
                    groups.append(row((t0 // blk) * blk + half - 1))
                else:
                    groups.append(jnp.where(sub < blk, row(t0 + half - 1), row(t0 + blk + half - 1)))
            bref = jnp.concatenate(groups, axis=0)
            e = jnp.where(upper, b - bref, bref - b)
        out.append(jnp.exp(e))
    return out, hi, lo


def _hgrn_body(q_ref, k_ref, v_ref, lg_ref, s0_ref, d_ref, lev_ref, o_ref, sout_ref, s_scr,
               *pad_scr, c, nlev, ntok, nsub):
    n = pl.program_id(0)
    r = pl.program_id(1)

    @pl.when(r == 0)
    def _():
        for sub in range(nsub):
            s_scr[sub] = s0_ref[sub if s0_ref.shape[0] == nsub else 0]

    if ntok < c:
        (pad,) = pad_scr

        @pl.when((n == 0) & (r == 0))
        def _():
            pad[...] = jnp.zeros_like(pad)

    tri = d_ref[...]
    lev = lev_ref[...]
    ones = jnp.ones((c, HG_DIM), BF16)
    tidx = lax.broadcasted_iota(jnp.int32, (c, HG_DIM), 0)
    for sub in range(nsub):
        rows = slice(sub * ntok, (sub + 1) * ntok)
        if ntok < c:
            for a, ref in enumerate((q_ref, k_ref, v_ref, lg_ref)):
                pad[sub, a, 0:ntok, :] = ref[rows, :]
            qr, kr, vr, lr = (pad.at[sub, a] for a in range(4))
        else:
            qr, kr, vr, lr = q_ref, k_ref, v_ref, lg_ref
        heads = [slice(h * HG_DIM, (h + 1) * HG_DIM) for h in range(HG_HEADS)]
        facs = [_decay_factors(lr[:, hs], tri, c, nlev) for hs in heads]
        xs = [f[0] for f in facs]
        lgs = [(f[1], f[2]) for f in facs]
        atts = []
        for hs, x in zip(heads, xs):
            q = qr[:, hs]
            k = kr[:, hs]
            att = jnp.where(lev == 0, _dot_nt(q.astype(BF16), k.astype(BF16)), 0.0)
            for l in range(1, nlev + 1):
                upper = ((tidx >> (l - 1)) & 1) == 1
                m = (jnp.where(upper, q, k) * x[1 + l]).astype(BF16)
                att = jnp.where(lev == l, _dot_nt(m, m), att)
            atts.append(att.astype(BF16))
        for h, (hs, x, att) in enumerate(zip(heads, xs, atts)):
            qd = (qr[:, hs] * x[0]).astype(BF16)
            o = _dot(qd, s_scr[sub, h].astype(BF16)) + _dot(att, vr[:, hs].astype(BF16))
            o_ref[rows, hs] = o[0:ntok]
        for h, (hs, x, (hi, lo)) in enumerate(zip(heads, xs, lgs)):
            kd = (kr[:, hs] * x[1]).astype(BF16)
            gl = _dot_tn(hi, ones) + _dot_tn(lo, ones)
            s_scr[sub, h] = jnp.exp(gl) * s_scr[sub, h] + _dot_tn(kd, vr[:, hs].astype(BF16))

    @pl.when(r == pl.num_programs(1) - 1)
    def _():
        sout_ref[...] = s_scr[...]


def _hgrn(proj, lg, s0, nseq, nchunks, c, row0, ntok=None):
    ntok = c if ntok is None else ntok
    dnp, levnp, nlev = _hgrn_consts(c)
    assert ntok == c or nchunks == 1
    nsub = 1 if ntok == c else SUBLANES // ntok
    step_rows = nsub * ntok
    assert row0 % step_rows == 0 and nseq % nsub == 0 and step_rows % SUBLANES == 0
    blk0 = row0 // step_rows
    tok_specs = [pl.BlockSpec((None, step_rows, D_MODEL),
                              functools.partial(lambda n, r, s: (s, blk0 + n * nchunks + r, 0), s=s))
                 for s in range(3)]
    lg_spec = pl.BlockSpec((step_rows, D_MODEL), lambda n, r: (blk0 + n * nchunks + r, 0))
    o_spec = pl.BlockSpec((step_rows, D_MODEL), lambda n, r: (n * nchunks + r, 0))
    o_shape = jax.ShapeDtypeStruct((nseq * nchunks * ntok, D_MODEL), F32)
    scratch = [] if ntok == c else [pltpu.VMEM((nsub, 4, c, D_MODEL), F32)]
    st_blk = (nsub, HG_HEADS, HG_DIM, HG_DIM)
    if s0.shape[0] == 1:
        s0_spec = pl.BlockSpec((1,) + st_blk[1:], lambda n, r: (0, 0, 0, 0))
    else:
        s0_spec = pl.BlockSpec(st_blk, lambda n, r: (n, 0, 0, 0))
    return pl.pallas_call(
        functools.partial(_hgrn_body, c=c, nlev=nlev, ntok=ntok, nsub=nsub),
        grid=(nseq // nsub, nchunks),
        in_specs=tok_specs + [
            lg_spec, s0_spec,
            pl.BlockSpec(dnp.shape, lambda n, r: (0, 0)),
            pl.BlockSpec(levnp.shape, lambda n, r: (0, 0)),
        ],
        out_specs=[o_spec, pl.BlockSpec(st_blk, lambda n, r: (n, 0, 0, 0))],
        out_shape=[o_shape, jax.ShapeDtypeStruct((nseq, HG_HEADS, HG_DIM, HG_DIM), F32)],
        scratch_shapes=[pltpu.VMEM(st_blk, F32)] + scratch,
        compiler_params=_params("arbitrary", "arbitrary"),
        name=f"hgrn_c{c}_t{ntok}",
    )(proj, proj, proj, lg, s0, jnp.asarray(dnp, BF16), jnp.asarray(levnp))


def _s5_compact(a_re, a_im, log_step, b_re, b_im, c_re, c_im, d_skip, nmax):
    hp = lax.Precision.HIGHEST
    lam_re = jnp.minimum(a_re.astype(F32), SSM_MIN_RE)
    lam_im = a_im.astype(F32)
    delta = jnp.exp(log_step.astype(F32))[:, None]
    n = jnp.arange(nmax + 1, dtype=F32)[:, None, None]
    mag = jnp.exp(lam_re * delta * n)
    ang = lam_im * delta * n
    pw_re = mag * jnp.cos(ang)
    pw_im = mag * jnp.sin(ang)
    x = pw_re[1] - 1.0
    y = pw_im[1]
    den = lam_re * lam_re + lam_im * lam_im
    cr = (x * lam_re + y * lam_im) / den
    ci = (y * lam_re - x * lam_im) / den
    bt_re = jnp.swapaxes(b_re.astype(F32), 1, 2)
    bt_im = jnp.swapaxes(b_im.astype(F32), 1, 2)
    bb_re = cr[:, None, :] * bt_re - ci[:, None, :] * bt_im
    bb_im = cr[:, None, :] * bt_im + ci[:, None, :] * bt_re
    w_re = pw_re[:nmax, :, None, :] * bb_re[None] - pw_im[:nmax, :, None, :] * bb_im[None]
    w_im = pw_re[:nmax, :, None, :] * bb_im[None] + pw_im[:nmax, :, None, :] * bb_re[None]
    cc_re = c_re.astype(F32)
    cc_im = c_im.astype(F32)
    kl = (jnp.einsum('gcp,ngdp->ngdc', cc_re, w_re, precision=hp)
          - jnp.einsum('gcp,ngdp->ngdc', cc_im, w_im, precision=hp))
    kl = kl.at[0].add(d_skip.astype(F32).reshape(SSM_G, SSM_GROUP)[:, :, None]
                      * jnp.eye(SSM_GROUP, dtype=F32)[None])
    ct_re = jnp.swapaxes(cc_re, 1, 2)
    ct_im = jnp.swapaxes(cc_im, 1, 2)
    p_re = pw_re[1:nmax + 1, :, :, None]
    p_im = pw_im[1:nmax + 1, :, :, None]
    wo_re = ct_re[None] * p_re - ct_im[None] * p_im
    wo_im = -(ct_re[None] * p_im + ct_im[None] * p_re)
    flat = lambda a: a.reshape(nmax, -1, a.shape[-1])
    return dict(kl=flat(kl), w_re=flat(w_re), w_im=flat(w_im), wo_re=flat(wo_re), wo_im=flat(wo_im),
                pw_re=pw_re, pw_im=pw_im)


def _s5_operators(cp, ntok):
    hp = lax.Precision.HIGHEST
    nb = SSM_G // S5_GB

    def block_diag(piece, rows_per_group):
        nn, rows, w = piece.shape
        rep = jnp.asarray(np.tile(np.eye(w, dtype=np.float32), (1, S5_GB)))
        row_g = (np.arange(rows) // rows_per_group) % S5_GB
        col_g = np.arange(S5_GB * w) // w
        mask = jnp.asarray((row_g[:, None] == col_g[None, :]).astype(np.float32))
        out = jnp.einsum('nrw,wl->nrl', piece, rep, precision=hp) * mask[None]
        return out.reshape(nn, nb, S5_GB * rows_per_group, S5_GB * w)

    bd_kl = block_diag(cp['kl'][:ntok], SSM_GROUP)
    zero = jnp.zeros_like(bd_kl[0])
    tz = jnp.stack([jnp.concatenate([bd_kl[t - s] if t >= s else zero for t in range(ntok)], axis=-1)
                    for s in range(ntok)], axis=1)
    state_in = lambda w: jnp.swapaxes(jnp.flip(block_diag(w[:ntok], SSM_GROUP), axis=0), 0, 1)
    state_out = lambda w: jnp.concatenate(list(block_diag(w[:ntok], SSM_P)), axis=-1)
    apow = jnp.stack([cp['pw_re'][ntok].reshape(nb, S5_GB * SSM_P),
                      cp['pw_im'][ntok].reshape(nb, S5_GB * SSM_P)], axis=1)
    pair = lambda a: a.reshape(nb, ntok // 2, 2 * LANES, a.shape[-1])
    tz = pair(tz)
    state_in = lambda w, f=state_in: pair(f(w))
    return dict(tz=tz.astype(BF16), wst_re=state_in(cp['w_re']), wst_im=state_in(cp['w_im']),
                wo_re=state_out(cp['wo_re']).astype(BF16), wo_im=state_out(cp['wo_im']).astype(BF16),
                apow=apow)


def _s5_body(u_ref, tz_ref, wre_ref, wim_ref, wore_ref, woim_ref, a_ref, x0re_ref, x0im_ref,
             y_ref, xre_ref, xim_ref, zre_scr, zim_scr, ub_scr,
             *, ntok, nchunks, nseq, npass):
    rows = nseq * nchunks
    nlb = (S5_GB * SSM_P) // LANES
    lane_blk = lambda lb: slice(lb * LANES, (lb + 1) * LANES)
    for sp in range(ntok // 2):
        us = jnp.concatenate([u_ref[pl.ds(2 * sp + d, rows, stride=ntok), :] for d in range(2)],
                             axis=1)
        uh, ul = _split_bf16(us)
        ub_scr[sp] = uh
        for w_ref, z_scr in ((wre_ref, zre_scr), (wim_ref, zim_scr)):
            wh, wl = _split_bf16(w_ref[sp])
            z = _dot(uh, wh)
            if npass >= 2:
                z = z + _dot(ul, wh)
            if npass >= 3:
                z = z + _dot(uh, wl)
            for lb in range(nlb):
                if sp == 0:
                    z_scr[lb] = z[:, lane_blk(lb)]
                else:
                    z_scr[lb] += z[:, lane_blk(lb)]
    a_re = a_ref[0:1, :]
    a_im = a_ref[1:2, :]

    def step(r, carry):
        x_re, x_im = carry
        at = pl.ds(r, nseq, stride=nchunks) if nchunks > 1 else pl.ds(0, nseq)
        z_re = jnp.concatenate([zre_scr[lb, at, :] for lb in range(nlb)], axis=1)
        z_im = jnp.concatenate([zim_scr[lb, at, :] for lb in range(nlb)], axis=1)
        for lb in range(nlb):
            zre_scr[lb, at, :] = x_re[:, lane_blk(lb)]
            zim_scr[lb, at, :] = x_im[:, lane_blk(lb)]
        return (a_re * x_re - a_im * x_im + z_re, a_re * x_im + a_im * x_re + z_im)

    x_re, x_im = lax.fori_loop(0, nchunks, step, (x0re_ref[0:nseq, :], x0im_ref[0:nseq, :]))
    xre_ref[...] = jnp.zeros_like(xre_ref)
    xim_ref[...] = jnp.zeros_like(xim_ref)
    xre_ref[0:nseq, :] = x_re
    xim_ref[0:nseq, :] = x_im
    for t0 in range(0, ntok, 2):
        cols = slice(t0 * LANES, (t0 + 2) * LANES)
        acc = jnp.zeros((rows, 2 * LANES), F32)
        for lb in range(0, nlb, 2):
            krows = slice(lb * LANES, (lb + 2) * LANES)
            for z_scr, wo_ref in ((zre_scr, wore_ref), (zim_scr, woim_ref)):
                xin = jnp.concatenate([z_scr[lb], z_scr[lb + 1]], axis=1).astype(BF16)
                acc = acc + _dot(xin, wo_ref[krows, cols])
        for sp in range(t0 // 2 + 1):
            acc = acc + _dot(ub_scr[sp], tz_ref[sp, :, cols])
        y_ref[pl.ds(t0, rows, stride=ntok), :] = acc[:, :LANES]
        y_ref[pl.ds(t0 + 1, rows, stride=ntok), :] = acc[:, LANES:]


def _s5(u, slot, ops, x0_re, x0_im, row0, nblk, nseq, nchunks, ntok, npass):
    nb = SSM_G // S5_GB
    rows = nseq * nchunks
    trows = rows * ntok
    assert row0 % trows == 0 and ntok % 2 == 0
    blk0 = row0 // trows
    nseq_pad = x0_re.shape[1]
    lanes = S5_GB * SSM_P
    if u.ndim == 3:
        u_spec = pl.BlockSpec((None, trows, LANES), lambda g, sb: (slot, blk0 + sb, g))
    else:
        u_spec = pl.BlockSpec((trows, LANES), lambda g, sb: (blk0 + sb, g))
    per_g = lambda shape: pl.BlockSpec((None,) + shape, lambda g, sb: (g,) + (0,) * len(shape))
    st_spec = pl.BlockSpec((None, nseq_pad, lanes), lambda g, sb: (sb, 0, g))
    st_shape = jax.ShapeDtypeStruct((nblk, nseq_pad, SSM_G * SSM_P), F32)
    return pl.pallas_call(
        functools.partial(_s5_body, ntok=ntok, nchunks=nchunks, nseq=nseq, npass=npass),
        grid=(nb, nblk),
        in_specs=[
            u_spec,
            per_g((ntok // 2, 2 * LANES, ntok * LANES)),
            per_g((ntok // 2, 2 * LANES, lanes)), per_g((ntok // 2, 2 * LANES, lanes)),
            per_g((lanes, ntok * LANES)), per_g((lanes, ntok * LANES)),
            per_g((2, lanes)),
            st_spec, st_spec,
        ],
        out_specs=[pl.BlockSpec((trows, LANES), lambda g, sb: (sb, g)), st_spec, st_spec],
        out_shape=[jax.ShapeDtypeStruct((nblk * trows, D_MODEL), F32), st_shape, st_shape],
        scratch_shapes=[pltpu.VMEM((lanes // LANES, rows, LANES), F32),
                        pltpu.VMEM((lanes // LANES, rows, LANES), F32),
                        pltpu.VMEM((ntok // 2, rows, 2 * LANES), BF16)],
        compiler_params=_params("arbitrary", "arbitrary"),
        name=f"s5_r{rows}_t{ntok}",
    )(u, ops['tz'], ops['wst_re'], ops['wst_im'], ops['wo_re'], ops['wo_im'], ops['apow'],
      x0_re, x0_im)


def _post_body(xa_ref, xb_ref, oa_ref, ob_ref, ya_ref, yb_ref, sg_ref, sa_ref, sb_ref, gnw_ref,
               wglu_ref, bglu_ref, wa_ref, wb_ref, wout_ref, out_ref, a_scr, *, n_first):
    first = pl.program_id(0) < n_first
    pick = lambda a_ref, b_ref: jnp.where(first, a_ref[...], b_ref[...])
    gnw = gnw_ref[...]
    o = pick(oa_ref, ob_ref)
    for h in range(HG_HEADS):
        hs = slice(h * HG_DIM, (h + 1) * HG_DIM)
        oh = o[:, hs]
        ms = jnp.mean(oh * oh, axis=-1, keepdims=True)
        a_scr[:, hs] = (oh * lax.rsqrt(ms + EPS) * gnw * sg_ref[:, hs]).astype(BF16)
    br_a = _dot(a_scr[...], wa_ref[...])
    y = _gelu_exact(pick(ya_ref, yb_ref))
    y = y * _sigmoid(_dot(y.astype(BF16), wglu_ref[...]) + bglu_ref[...])
    br_b = _dot(y.astype(BF16), wb_ref[...])
    mixed = sa_ref[...] * br_a + sb_ref[...] * br_b
    out_ref[...] = pick(xa_ref, xb_ref) + _dot(mixed.astype(BF16), wout_ref[...])


def _post(xa, xb, oa, ob, ya, yb, proj, g_norm_w, w_glu, b_glu, w_a, w_b, w_out, tm):
    t = xa.shape[0] + xb.shape[0]
    n_first = xa.shape[0] // tm
    assert xa.shape[0] % tm == 0 and xb.shape[0] % tm == 0 and proj.shape[1] == t
    spec_a, spec_b = _two_source_specs(tm, n_first)
    slot = lambda s: pl.BlockSpec((None, tm, D_MODEL), lambda i: (s, i, 0))
    full = lambda shape: pl.BlockSpec(shape, lambda i: (0,) * len(shape))
    sq = (D_MODEL, D_MODEL)
    return pl.pallas_call(
        functools.partial(_post_body, n_first=n_first),
        grid=(t // tm,),
        in_specs=[spec_a, spec_b, spec_a, spec_b, spec_a, spec_b, slot(3), slot(5), slot(6),
                  full((1, HG_DIM)), full(sq), full((1, D_MODEL)), full(sq), full(sq), full(sq)],
        out_specs=pl.BlockSpec((tm, D_MODEL), lambda i: (i, 0)),
        out_shape=jax.ShapeDtypeStruct((t, D_MODEL), F32),
        scratch_shapes=[pltpu.VMEM((tm, D_MODEL), BF16)],
        compiler_params=_params("arbitrary"),
        name="post",
    )(xa, xb, oa, ob, ya, yb, proj, proj, proj, g_norm_w.reshape(1, HG_DIM), w_glu.astype(BF16),
      b_glu.reshape(1, D_MODEL), w_a.astype(BF16), w_b.astype(BF16), w_out.astype(BF16))


def _sort_network(n):
    pairs = []
    p = 1
    while p < n:
        k = p
        while k >= 1:
            for j in range(k % p, n - k, 2 * k):
                for i in range(min(k, n - j - k)):
                    if (i + j) // (2 * p) == (i + j + k) // (2 * p):
                        pairs.append((i + j, i + j + k))
            k //= 2
        p *= 2
    return pairs


def _exchange(vs, i, j):
    vs[i], vs[j] = jnp.maximum(vs[i], vs[j]), jnp.minimum(vs[i], vs[j])


def _merge_top(a, b):
    n = len(a)
    vs = [jnp.maximum(a[i], b[n - 1 - i]) for i in range(n)]
    d = n // 2
    while d >= 1:
        for i in range(n):
            if i & d == 0:
                _exchange(vs, i, i + d)
        d //= 2
    return vs


def _merge_sublanes(vs):
    shift = SUBLANES // 2
    while shift >= 1:
        vs = _merge_top(vs, [pltpu.roll(v, shift, axis=0) for v in vs])
        shift //= 2
    return vs


def _top16_of_keys(s):
    vs = [s[v * SUBLANES:(v + 1) * SUBLANES] for v in range(PEER_NKEYS // SUBLANES)]
    for i, j in _sort_network(len(vs)):
        _exchange(vs, i, j)
    vs = _merge_sublanes(vs)
    return jnp.concatenate([v[0:1] for v in vs], axis=0)


_CAND_COUNTS = tuple(PEER_TOPK // (a + 1) for a in range(PEER_TOPK))


def _candidate_thresholds(v1, v2):
    sub = lax.broadcasted_iota(jnp.int32, (SUBLANES, LANES), 0)
    halves = []
    for half in range(PEER_TOPK // SUBLANES):
        v2h = v2[half * SUBLANES:(half + 1) * SUBLANES]
        halves.append([jnp.where(sub + half * SUBLANES < cnt, v1[a:a + 1] + v2h, -jnp.inf)
                       for a, cnt in enumerate(_CAND_COUNTS)])
    top = _merge_sublanes(_merge_top(halves[0], halves[1]))
    c16 = top[PEER_TOPK - 1][0:1]
    cands = halves[0] + halves[1]
    below = functools.reduce(jnp.maximum, [jnp.where(c < c16, c, -jnp.inf) for c in cands])
    c17 = jnp.max(below, axis=0, keepdims=True)
    return cands, c16, c17


def _peer_body(x_ref, n2_ref, fn_ref, wq_ref, k1_ref, k2_ref, u_ref, vt_ref, ya_ref, yb_ref,
               hnt_scr, s_scr, p_scr, tau_scr, vs_scr, act0, act1, g0, g1, acc_scr,
               *, tm, eb, n_first, nblk):
    i = pl.program_id(0)
    j = pl.program_id(1)
    nlt = tm // LANES
    nrow = eb // PEER_NKEYS
    assert nrow == SUBLANES

    @pl.when(j == 0)
    def _():
        acc_scr[...] = jnp.zeros_like(acc_scr)
        x = x_ref[...]
        ms = jnp.mean(x * x, axis=-1, keepdims=True)
        hn32 = x * lax.rsqrt(ms + EPS) * n2_ref[...]
        hnt_scr[...] = hn32.T.astype(BF16)
        qall = _dot(hn32.astype(BF16), wq_ref[...]).astype(BF16)
        for h in range(PEER_HEADS):
            for half, k_ref in enumerate((k1_ref, k2_ref)):
                c0 = (2 * h + half) * PEER_HALF
                sc = _dot_nt(k_ref[h], qall[:, c0:c0 + PEER_HALF])
                for lt in range(nlt):
                    s_scr[2 * h + half, lt] = sc[:, lt * LANES:(lt + 1) * LANES]

        def top_body(idx, carry):
            hh = idx // nlt
            lt = idx % nlt
            vs_scr[hh, lt] = _top16_of_keys(s_scr[hh, lt])
            return carry

        lax.fori_loop(0, 2 * PEER_HEADS * nlt, top_body, 0)

        def gate_body(idx, carry):
            h = idx // nlt
            lt = idx % nlt
            v1 = vs_scr[2 * h, lt]
            v2 = vs_scr[2 * h + 1, lt]
            cands, c16, c17 = _candidate_thresholds(v1, v2)
            top = v1[0:1] + v2[0:1]
            z = functools.reduce(jnp.add, [jnp.where(c >= c16, jnp.exp(c - top), 0.0) for c in cands])
            z = jnp.sum(z, axis=0, keepdims=True)
            tau_scr[h, lt] = jnp.broadcast_to(0.5 * (c16 + c17), (SUBLANES, LANES))
            s1 = s_scr[2 * h, lt]
            s2 = s_scr[2 * h + 1, lt]
            p_scr[2 * h, lt] = jnp.where(s1 >= v1[PEER_TOPK - 1:], jnp.exp(s1 - v1[0:1]), 0.0) / z
            p_scr[2 * h + 1, lt] = jnp.where(s2 >= v2[PEER_TOPK - 1:], jnp.exp(s2 - v2[0:1]), 0.0)
            return carry

        lax.fori_loop(0, PEER_HEADS * nlt, gate_body, 0)

    e1_rows = pl.ds(pl.multiple_of(jnp.clip(j - 1, 0, nblk - 1) * nrow, nrow), nrow)
    half_keys = PEER_NKEYS // 2

    def stages(act_new, act_cur, g_new, g_cur, first=True, middle=True, final=True):
        def pre_activation(rows):
            act_new[rows, :] = _dot(u_ref[rows, :], hnt_scr[...])

        def accumulate(rows):
            acc_scr[rows, :] += _dot(vt_ref[rows, :], g_cur[...])

        def gated(lt, r0, kh):
            cols = slice(lt * LANES, (lt + 1) * LANES)
            keys = slice(kh * half_keys, (kh + 1) * half_keys)
            ws = [jnp.zeros((half_keys, LANES), F32) for _ in range(2)]
            for h in range(PEER_HEADS):
                s2 = s_scr[2 * h + 1, lt, keys, :]
                p2 = p_scr[2 * h + 1, lt, keys, :]
                s1grp = s_scr[2 * h, lt, e1_rows, :]
                p1grp = p_scr[2 * h, lt, e1_rows, :]
                tau = tau_scr[h, lt][0:1]
                for d in range(2):
                    theta = tau - s1grp[r0 + d:r0 + d + 1]
                    ws[d] = ws[d] + jnp.where(s2 >= theta, p2, 0.0) * p1grp[r0 + d:r0 + d + 1]
            for d in range(2):
                rows = slice((r0 + d) * PEER_NKEYS + kh * half_keys, (r0 + d) * PEER_NKEYS + (kh + 1) * half_keys)
                g_new[rows, cols] = (ws[d] * _gelu_exact(act_cur[rows, cols])).astype(BF16)

        chunk = 4 * LANES
        matmul_jobs = []
        if first:
            matmul_jobs += [functools.partial(pre_activation, slice(m, m + chunk)) for m in range(0, eb, chunk)]
        if final:
            matmul_jobs += [functools.partial(accumulate, slice(m, m + chunk)) for m in range(0, D_MODEL, chunk)]
        gated_jobs = []
        if middle:
            gated_jobs = [functools.partial(gated, lt, r0, kh)
                          for lt in range(nlt) for r0 in range(0, nrow, 2) for kh in range(2)]
        per = -(-len(gated_jobs) // len(matmul_jobs))
        for k, job in enumerate(matmul_jobs):
            job()
            for gjob in gated_jobs[k * per:(k + 1) * per]:
                gjob()

    nstep = nblk + 2
    assert nblk % 2 == 0 and nblk >= 4
    steady = (j >= 2) & (j < nblk)

    @pl.when(j == 0)
    def _():
        stages(act0, act1, g0, g1, middle=False, final=False)

    @pl.when(j == 1)
    def _():
        stages(act1, act0, g1, g0, final=False)

    @pl.when(steady & (j % 2 == 0))
    def _():
        stages(act0, act1, g0, g1)

    @pl.when(steady & (j % 2 == 1))
    def _():
        stages(act1, act0, g1, g0)

    @pl.when(j == nstep - 2)
    def _():
        stages(act0, act1, g0, g1, first=False)

    @pl.when(j == nstep - 1)
    def _():
        stages(act1, act0, g1, g0, first=False, middle=False)

    def finish(y_ref):
        xo = x_ref[...] + acc_scr[...].T
        ms = jnp.mean(xo * xo, axis=-1, keepdims=True)
        y_ref[...] = xo * lax.rsqrt(ms + EPS) * fn_ref[...]

    last = j == pl.num_programs(1) - 1

    @pl.when(last & (i < n_first))
    def _():
        finish(ya_ref)

    @pl.when(last & (i >= n_first))
    def _():
        finish(yb_ref)


def _peer(x, n_first_rows, norm2_w, final_norm_w, wq, k1, k2, u_tab, v_tab, tm, eb):
    t = x.shape[0]
    assert t % tm == 0 and n_first_rows % tm == 0
    n_first = n_first_rows // tm
    nexp = u_tab.shape[0]
    assert nexp == PEER_NKEYS * PEER_NKEYS and nexp % eb == 0
    nblk = nexp // eb
    qw = 2 * PEER_HEADS * PEER_HALF
    nlt = tm // LANES
    full = lambda shape: pl.BlockSpec(shape, lambda i, j: (0,) * len(shape))
    out_a, out_b = _two_source_specs(tm, n_first)
    return pl.pallas_call(
        functools.partial(_peer_body, tm=tm, eb=eb, n_first=n_first, nblk=nblk),
        grid=(t // tm, nblk + 2),
        in_specs=[
            pl.BlockSpec((tm, D_MODEL), lambda i, j: (i, 0)),
            full((1, D_MODEL)), full((1, D_MODEL)),
            full((D_MODEL, qw)),
            full((PEER_HEADS, PEER_NKEYS, PEER_HALF)), full((PEER_HEADS, PEER_NKEYS, PEER_HALF)),
            pl.BlockSpec((eb, D_MODEL), lambda i, j: (jnp.minimum(j, nblk - 1), 0)),
            pl.BlockSpec((None, D_MODEL, eb), lambda i, j: (jnp.clip(j - 2, 0, nblk - 1), 0, 0)),
        ],
        out_specs=[out_a, out_b],
        out_shape=[jax.ShapeDtypeStruct((n_first_rows, D_MODEL), F32),
                   jax.ShapeDtypeStruct((t - n_first_rows, D_MODEL), F32)],
        scratch_shapes=[
            pltpu.VMEM((D_MODEL, tm), BF16),
            pltpu.VMEM((2 * PEER_HEADS, nlt, PEER_NKEYS, LANES), F32),
            pltpu.VMEM((2 * PEER_HEADS, nlt, PEER_NKEYS, LANES), F32),
            pltpu.VMEM((PEER_HEADS, nlt, SUBLANES, LANES), F32),
            pltpu.VMEM((2 * PEER_HEADS, nlt, PEER_TOPK, LANES), F32),
            pltpu.VMEM((eb, tm), F32), pltpu.VMEM((eb, tm), F32),
            pltpu.VMEM((eb, tm), BF16), pltpu.VMEM((eb, tm), BF16),
            pltpu.VMEM((D_MODEL, tm), F32),
        ],
        compiler_params=_params("arbitrary", "arbitrary"),
        name="peer",
    )(x, norm2_w.reshape(1, D_MODEL), final_norm_w.reshape(1, D_MODEL), wq.astype(BF16),
      k1.astype(BF16), k2.astype(BF16), u_tab.astype(BF16),
      jnp.transpose(v_tab.astype(BF16).reshape(nblk, eb, D_MODEL), (0, 2, 1)))


def kernel(x_prompt, x_sample, state_hgrn, state_ssm_re, state_ssm_im, meta_tokens, lower_bounds,
           norm1_w, w_in, g_norm_w, ssm_a_re, ssm_a_im, ssm_log_step, ssm_b_re, ssm_b_im,
           ssm_c_re, ssm_c_im, ssm_d, w_glu, b_glu, w_branch_a, w_branch_b, w_out, norm2_w,
           peer_wq, peer_k1, peer_k2, peer_u, peer_v, final_norm_w):
    depth = w_in.shape[0]
    assert depth == 1, "single-layer step only"
    nb, seq, _ = x_prompt.shape
    ns, dseq, _ = x_sample.shape
    tp = nb * seq
    tsm = ns * dseq
    tm = 512
    hg_c = 128
    small_c = 16
    s5_tok = 8
    s5_seq = 4
    assert tp % tm == 0 and tsm % tm == 0 and seq % hg_c == 0 and seq % s5_tok == 0
    assert nb % s5_seq == 0 and dseq <= small_c and dseq % 2 == 0 and N_META % s5_tok == 0

    w_in_b = w_in[0].astype(BF16)
    xp = x_prompt.reshape(tp, D_MODEL).astype(F32)
    xs = x_sample.reshape(tsm, D_MODEL).astype(F32)
    proj, lg = _inproj(xp, xs, norm1_w[0], lower_bounds, w_in_b, tm)
    proj_m, lg_m = _inproj(meta_tokens.astype(F32), None, norm1_w[0], lower_bounds, w_in_b, N_META)

    zero_hg = jnp.zeros((1, HG_HEADS, HG_DIM, HG_DIM), F32)
    _, hg_meta = _hgrn(proj_m, lg_m, zero_hg, 1, 1, N_META, 0)
    o_p, hg_p = _hgrn(proj, lg, hg_meta, nb, seq // hg_c, hg_c, 0)
    o_s, hg_s = _hgrn(proj, lg, state_hgrn[0].astype(F32), ns, 1, small_c, tp, ntok=dseq)
    o_s = o_s.reshape(tsm, D_MODEL)

    ssm = (ssm_a_re[0], ssm_a_im[0], ssm_log_step[0], ssm_b_re[0], ssm_b_im[0], ssm_c_re[0],
           ssm_c_im[0], ssm_d[0])
    compact = _s5_compact(*ssm, max(s5_tok, dseq))
    ops_p = _s5_operators(compact, s5_tok)
    ops_s = _s5_operators(compact, dseq)
    nst = SSM_G * SSM_P
    u_meta = jnp.pad(proj_m[4], ((0, 7 * N_META), (0, 0)))
    zero_ss = jnp.zeros((1, SUBLANES, nst), F32)
    _, mre, mim = _s5(u_meta, 0, ops_p, zero_ss, zero_ss, 0, 1, SUBLANES, N_META // s5_tok, s5_tok, 3)
    nblk = nb // s5_seq
    x0re = jnp.broadcast_to(mre[:, 0:1], (nblk, SUBLANES, nst))
    x0im = jnp.broadcast_to(mim[:, 0:1], (nblk, SUBLANES, nst))
    y_p, pre, pim = _s5(proj, 4, ops_p, x0re, x0im, 0, nblk, s5_seq, seq // s5_tok, s5_tok, 1)
    y_s, sre, sim = _s5(proj, 4, ops_s, state_ssm_re[0].astype(F32).reshape(1, ns, nst),
                        state_ssm_im[0].astype(F32).reshape(1, ns, nst), tp, 1, ns, 1, dseq, 3)

    x1 = _post(xp, xs, o_p, o_s, y_p, y_s, proj, g_norm_w[0], w_glu[0], b_glu[0], w_branch_a[0],
               w_branch_b[0], w_out[0], tm)
    y_pr, y_sm = _peer(x1, tp, norm2_w[0], final_norm_w, peer_wq[0], peer_k1[0], peer_k2[0],
                       peer_u[0], peer_v[0], tm, SUBLANES * PEER_NKEYS)

    sd = state_hgrn.dtype
    st = lambda a: a[:, :s5_seq].reshape(nb, SSM_G, SSM_P)[None]
    return (y_pr.reshape(nb, seq, D_MODEL).astype(x_prompt.dtype),
            y_sm.reshape(ns, dseq, D_MODEL).astype(x_sample.dtype),
            hg_p[None].astype(sd),
            st(pre).astype(state_ssm_re.dtype),
            st(pim).astype(state_ssm_im.dtype),
            hg_s[None].astype(sd),
            sre.reshape(1, ns, SSM_G, SSM_P).astype(state_ssm_re.dtype),
            sim.reshape(1, ns, SSM_G, SSM_P).astype(state_ssm_im.dtype))
```

```python
import functools
import math

import numpy as np
import jax
import jax.numpy as jnp
from jax import lax
from jax.experimental import pallas as pl
from jax.experimental.pallas import tpu as pltpu

F32 = jnp.float32
BF16 = jnp.bfloat16

D_MODEL = 1024
N_META = 16
HG_HEADS = 8
HG_DIM = 128
SSM_G = 64
SSM_GROUP = 16
SSM_P = 64
SSM_MIN_RE = -1e-4
S5_GB = 8
PEER_HEADS = 8
PEER_NKEYS = 128
PEER_TOPK = 16
PEER_HALF = 128
EPS = 1e-6
LANES = 128
SUBLANES = 8
VMEM_LIMIT = 56 * 1024 * 1024


def _dot(a, b):
    return jnp.dot(a, b, preferred_element_type=F32)


def _dot_nt(a, b):
    return lax.dot_general(a, b, (((1,), (1,)), ((), ())), preferred_element_type=F32)


def _dot_tn(a, b):
    return lax.dot_general(a, b, (((0,), (0,)), ((), ())), preferred_element_type=F32)


def _split_bf16(x):
    hi = x.astype(BF16)
    lo = (x - hi.astype(F32)).astype(BF16)
    return hi, lo


def _sigmoid(x):
    return 1.0 / (1.0 + jnp.exp(-x))


def _gelu_exact(x):
    return 0.5 * x * (1.0 + lax.erf(x * (1.0 / math.sqrt(2.0))))


def _params(*sem):
    return pltpu.CompilerParams(dimension_semantics=sem, vmem_limit_bytes=VMEM_LIMIT)


def _two_source_specs(tm, n_first, ncols=D_MODEL):
    first = pl.BlockSpec((tm, ncols), lambda i, *_: (jnp.minimum(i, n_first - 1), 0))
    second = pl.BlockSpec((tm, ncols), lambda i, *_: (jnp.maximum(i - n_first, 0), 0))
    return first, second


N_SEG = 7


def _inproj_body(xa_ref, xb_ref, nw_ref, lbs_ref, w_ref, o_ref, lg_ref, *, n_first):
    x = jnp.where(pl.program_id(0) < n_first, xa_ref[...], xb_ref[...])
    ms = jnp.mean(x * x, axis=-1, keepdims=True)
    h = (x * lax.rsqrt(ms + EPS) * nw_ref[...]).astype(BF16)
    lbs = lbs_ref[...]
    e = jnp.exp(lbs - jnp.max(lbs, axis=0, keepdims=True))
    lb = e[0:1] / jnp.sum(e, axis=0, keepdims=True)
    for seg in range(N_SEG):
        p = _dot(h, w_ref[:, seg * D_MODEL:(seg + 1) * D_MODEL])
        if seg in (0, 3):
            o_ref[seg] = p * _sigmoid(p)
        elif seg == 1:
            fg = lb + (1.0 - lb) * _sigmoid(p)
            o_ref[seg] = 1.0 - fg
            lg_ref[...] = jnp.log(fg)
        elif seg in (2, 4):
            o_ref[seg] = p
        else:
            o_ref[seg] = _sigmoid(p)


def _inproj(xa, xb, norm_w, lower_bounds, w_in_bf16, tm):
    if xb is None:
        xb = xa
        t = xa.shape[0]
        n_first = t // tm
    else:
        t = xa.shape[0] + xb.shape[0]
        n_first = xa.shape[0] // tm
        assert xa.shape[0] % tm == 0 and xb.shape[0] % tm == 0
    spec_a, spec_b = _two_source_specs(tm, n_first)
    return pl.pallas_call(
        functools.partial(_inproj_body, n_first=n_first),
        grid=(t // tm,),
        in_specs=[
            spec_a, spec_b,
            pl.BlockSpec((1, D_MODEL), lambda i: (0, 0)),
            pl.BlockSpec(lower_bounds.shape, lambda i: (0, 0)),
            pl.BlockSpec((D_MODEL, N_SEG * D_MODEL), lambda i: (0, 0)),
        ],
        out_specs=[
            pl.BlockSpec((N_SEG, tm, D_MODEL), lambda i: (0, i, 0)),
            pl.BlockSpec((tm, D_MODEL), lambda i: (i, 0)),
        ],
        out_shape=[
            jax.ShapeDtypeStruct((N_SEG, t, D_MODEL), F32),
            jax.ShapeDtypeStruct((t, D_MODEL), F32),
        ],
        compiler_params=_params("arbitrary"),
        name="inproj",
    )(xa, xb, norm_w.reshape(1, D_MODEL), lower_bounds, w_in_bf16)


def _hgrn_consts(c):
    nlev = int(round(math.log2(c)))
    assert 1 << nlev == c and c % SUBLANES == 0
    tri = np.tril(np.ones((c, c), np.float32))
    lev = np.full((c, c), -1, np.int32)
    for t in range(c):
        for s in range(t + 1):
            lev[t, s] = (t ^ s).bit_length()
    return tri, lev, nlev


def _decay_factors(lg, tri, c, nlev):
    hi, lo = _split_bf16(lg)
    b = _dot(tri, hi) + _dot(tri, lo)
    row = lambda r: jnp.broadcast_to(b[r:r + 1], (SUBLANES, HG_DIM))
    sub = lax.broadcasted_iota(jnp.int32, (SUBLANES, HG_DIM), 0)
    tidx = lax.broadcasted_iota(jnp.int32, (c, HG_DIM), 0)
    out = [jnp.exp(b), jnp.exp(b[c - 1:c] - b)]
    for l in range(1, nlev + 1):
        blk = 1 << l
        half = blk >> 1
        upper = ((tidx >> (l - 1)) & 1) == 1
        if l == 1:
            e = jnp.where(upper, lg, 0.0)
        else:
            groups = []
            for t0 in range(0, c, SUBLANES):
                if blk >= SUBLANES:
                    groups.append(row((t0 // blk) * blk + half - 1))
                else:
                    groups.append(jnp.where(sub < blk, row(t0 + half - 1), row(t0 + blk + half - 1)))
            bref = jnp.concatenate(groups, axis=0)
            e = jnp.where(upper, b - bref, bref - b)
        out.append(jnp.exp(e))
    return out, hi, lo


def _hgrn_body(q_ref, k_ref, v_ref, lg_ref, s0_ref, d_ref, lev_ref, o_ref, sout_ref, s_scr,
               *pad_scr, c, nlev, ntok, nsub):
    n = pl.program_id(0)
    r = pl.program_id(1)

    @pl.when(r == 0)
    def _():
        for sub in range(nsub):
            s_scr[sub] = s0_ref[sub if s0_ref.shape[0] == nsub else 0]

    if ntok < c:
        (pad,) = pad_scr

        @pl.when((n == 0) & (r == 0))
        def _():
            pad[...] = jnp.zeros_like(pad)

    tri = d_ref[...]
    lev = lev_ref[...]
    ones = jnp.ones((c, HG_DIM), BF16)
    tidx = lax.broadcasted_iota(jnp.int32, (c, HG_DIM), 0)
    for sub in range(nsub):
        rows = slice(sub * ntok, (sub + 1) * ntok)
        if ntok < c:
            for a, ref in enumerate((q_ref, k_ref, v_ref, lg_ref)):
                pad[sub, a, 0:ntok, :] = ref[rows, :]
            qr, kr, vr, lr = (pad.at[sub, a] for a in range(4))
        else:
            qr, kr, vr, lr = q_ref, k_ref, v_ref, lg_ref
        heads = [slice(h * HG_DIM, (h + 1) * HG_DIM) for h in range(HG_HEADS)]
        facs = [_decay_factors(lr[:, hs], tri, c, nlev) for hs in heads]
        xs = [f[0] for f in facs]
        lgs = [(f[1], f[2]) for f in facs]
        atts = []
        for hs, x in zip(heads, xs):
            q = qr[:, hs]
            k = kr[:, hs]
            att = jnp.where(lev == 0, _dot_nt(q.astype(BF16), k.astype(BF16)), 0.0)
            for l in range(1, nlev + 1):
                upper = ((tidx >> (l - 1)) & 1) == 1
                m = (jnp.where(upper, q, k) * x[1 + l]).astype(BF16)
                att = jnp.where(lev == l, _dot_nt(m, m), att)
            atts.append(att.astype(BF16))
        for h, (hs, x, att) in enumerate(zip(heads, xs, atts)):
            qd = (qr[:, hs] * x[0]).astype(BF16)
            o = _dot(qd, s_scr[sub, h].astype(BF16)) + _dot(att, vr[:, hs].astype(BF16))
            o_ref[rows, hs] = o[0:ntok]
        for h, (hs, x, (hi, lo)) in enumerate(zip(heads, xs, lgs)):
            kd = (kr[:, hs] * x[1]).astype(BF16)
            gl = _dot_tn(hi, ones) + _dot_tn(lo, ones)
            s_scr[sub, h] = jnp.exp(gl) * s_scr[sub, h] + _dot_tn(kd, vr[:, hs].astype(BF16))

    @pl.when(r == pl.num_programs(1) - 1)
    def _():
        sout_ref[...] = s_scr[...]


def _hgrn(proj, lg, s0, nseq, nchunks, c, row0, ntok=None):
    ntok = c if ntok is None else ntok
    dnp, levnp, nlev = _hgrn_consts(c)
    assert ntok == c or nchunks == 1
    nsub = 1 if ntok == c else (2 * SUBLANES) // ntok
    step_rows = nsub * ntok
    assert row0 % step_rows == 0 and nseq % nsub == 0 and step_rows % SUBLANES == 0
    blk0 = row0 // step_rows
    tok_specs = [pl.BlockSpec((None, step_rows, D_MODEL),
                              functools.partial(lambda n, r, s: (s, blk0 + n * nchunks + r, 0), s=s))
                 for s in range(3)]
    lg_spec = pl.BlockSpec((step_rows, D_MODEL), lambda n, r: (blk0 + n * nchunks + r, 0))
    o_spec = pl.BlockSpec((step_rows, D_MODEL), lambda n, r: (n * nchunks + r, 0))
    o_shape = jax.ShapeDtypeStruct((nseq * nchunks * ntok, D_MODEL), F32)
    scratch = [] if ntok == c else [pltpu.VMEM((nsub, 4, c, D_MODEL), F32)]
    st_blk = (nsub, HG_HEADS, HG_DIM, HG_DIM)
    if s0.shape[0] == 1:
        s0_spec = pl.BlockSpec((1,) + st_blk[1:], lambda n, r: (0, 0, 0, 0))
    else:
        s0_spec = pl.BlockSpec(st_blk, lambda n, r: (n, 0, 0, 0))
    return pl.pallas_call(
        functools.partial(_hgrn_body, c=c, nlev=nlev, ntok=ntok, nsub=nsub),
        grid=(nseq // nsub, nchunks),
        in_specs=tok_specs + [
            lg_spec, s0_spec,
            pl.BlockSpec(dnp.shape, lambda n, r: (0, 0)),
            pl.BlockSpec(levnp.shape, lambda n, r: (0, 0)),
        ],
        out_specs=[o_spec, pl.BlockSpec(st_blk, lambda n, r: (n, 0, 0, 0))],
        out_shape=[o_shape, jax.ShapeDtypeStruct((nseq, HG_HEADS, HG_DIM, HG_DIM), F32)],
        scratch_shapes=[pltpu.VMEM(st_blk, F32)] + scratch,
        compiler_params=_params("arbitrary", "arbitrary"),
        name=f"hgrn_c{c}_t{ntok}",
    )(proj, proj, proj, lg, s0, jnp.asarray(dnp, BF16), jnp.asarray(levnp))


def _s5_pieces_body(are_ref, aim_ref, ls_ref, btre_ref, btim_ref, ccre_ref, ccim_ref, d_ref,
                     kl_ref, wre_ref, wim_ref, wore_ref, woim_ref, pwre_ref, pwim_ref, *, nmax):
    hp = lax.Precision.HIGHEST
    rows = S5_GB * SSM_GROUP
    lanes = S5_GB * SSM_P
    lam_re = jnp.minimum(are_ref[...], SSM_MIN_RE)
    lam_im = aim_ref[...]
    delta = jnp.exp(ls_ref[...])
    pw_re, pw_im = [], []
    for n in range(nmax + 1):
        mag = jnp.exp(lam_re * delta * float(n))
        ang = lam_im * delta * float(n)
        pw_re.append(mag * jnp.cos(ang))
        pw_im.append(mag * jnp.sin(ang))
        pwre_ref[n] = pw_re[n]
        pwim_ref[n] = pw_im[n]
    x = pw_re[1] - 1.0
    y = pw_im[1]
    den = lam_re * lam_re + lam_im * lam_im
    cr = (x * lam_re + y * lam_im) / den
    ci = (y * lam_re - x * lam_im) / den
    bb_re = cr * btre_ref[...] - ci * btim_ref[...]
    bb_im = cr * btim_ref[...] + ci * btre_ref[...]
    cc_re = ccre_ref[...]
    cc_im = ccim_ref[...]
    row_g = lax.broadcasted_iota(jnp.int32, (rows, rows), 0) // SSM_GROUP
    same_c = row_g == lax.broadcasted_iota(jnp.int32, (rows, rows), 1) // SSM_GROUP
    eye = (lax.broadcasted_iota(jnp.int32, (rows, rows), 0)
           == lax.broadcasted_iota(jnp.int32, (rows, rows), 1)).astype(F32)
    same_p = (lax.broadcasted_iota(jnp.int32, (rows, lanes), 0) // SSM_GROUP
              == lax.broadcasted_iota(jnp.int32, (rows, lanes), 1) // SSM_P)
    rep = (lax.broadcasted_iota(jnp.int32, (SSM_P, lanes), 0)
           == lax.broadcasted_iota(jnp.int32, (SSM_P, lanes), 1) % SSM_P).astype(F32)
    spread = lambda t: jnp.where(same_p, jnp.dot(t, rep, precision=hp), 0.0)
    nt = lambda u, v: lax.dot_general(u, v, (((1,), (1,)), ((), ())), precision=hp)
    tn = lambda u, v: lax.dot_general(u, v, (((0,), (0,)), ((), ())), precision=hp)
    for n in range(nmax):
        w_re = pw_re[n] * bb_re - pw_im[n] * bb_im
        w_im = pw_re[n] * bb_im + pw_im[n] * bb_re
        kl = jnp.where(same_c, nt(w_re, cc_re) - nt(w_im, cc_im), 0.0)
        if n == 0:
            kl = kl + d_ref[...] * eye
        kl_ref[n] = kl
        wre_ref[n] = spread(w_re)
        wim_ref[n] = spread(w_im)
        o_re = cc_re * pw_re[n + 1] - cc_im * pw_im[n + 1]
        o_im = -(cc_re * pw_im[n + 1] + cc_im * pw_re[n + 1])
        wore_ref[n] = tn(spread(o_re), eye)
        woim_ref[n] = tn(spread(o_im), eye)


def _s5_pieces(a_re, a_im, log_step, b_re, b_im, c_re, c_im, d_skip, nmax):
    nb = SSM_G // S5_GB
    rows = S5_GB * SSM_GROUP
    lanes = S5_GB * SSM_P
    per_row = lambda a: jnp.repeat(a.astype(F32), SSM_GROUP, axis=0)
    flat = lambda a: a.astype(F32).reshape(SSM_G * SSM_GROUP, SSM_P)
    tile = pl.BlockSpec((rows, SSM_P), lambda g: (g, 0))
    out4 = lambda r, c: pl.BlockSpec((nmax, None, r, c), lambda g: (0, g, 0, 0))
    shape4 = lambda r, c: jax.ShapeDtypeStruct((nmax, nb, r, c), F32)
    pw_spec = pl.BlockSpec((nmax + 1, rows, SSM_P), lambda g: (0, g, 0))
    pw_shape = jax.ShapeDtypeStruct((nmax + 1, SSM_G * SSM_GROUP, SSM_P), F32)
    kl, w_re, w_im, wo_re, wo_im, pw_re, pw_im = pl.pallas_call(
        functools.partial(_s5_pieces_body, nmax=nmax),
        grid=(nb,),
        in_specs=[tile] * 7 + [pl.BlockSpec((rows, rows), lambda g: (g, 0))],
        out_specs=[out4(rows, rows), out4(rows, lanes), out4(rows, lanes), out4(lanes, rows),
                   out4(lanes, rows), pw_spec, pw_spec],
        out_shape=[shape4(rows, rows), shape4(rows, lanes), shape4(rows, lanes), shape4(lanes, rows),
                   shape4(lanes, rows), pw_shape, pw_shape],
        compiler_params=_params("arbitrary"),
        name="s5_pieces",
    )(per_row(a_re), per_row(a_im),
      per_row(jnp.broadcast_to(log_step.astype(F32)[:, None], (SSM_G, SSM_P))),
      flat(jnp.swapaxes(b_re, 1, 2)), flat(jnp.swapaxes(b_im, 1, 2)), flat(c_re), flat(c_im),
      jnp.broadcast_to(d_skip.astype(F32).reshape(SSM_G * SSM_GROUP, 1), (SSM_G * SSM_GROUP, rows)))
    per_group = lambda p: p[:, ::SSM_GROUP, :].reshape(nmax + 1, nb, lanes)
    return dict(kl=kl, w_re=w_re, w_im=w_im, wo_re=wo_re, wo_im=wo_im,
                pw_re=per_group(pw_re), pw_im=per_group(pw_im))


def _s5_operators(cp, ntok):
    nb = SSM_G // S5_GB
    kl = cp['kl']
    zero = jnp.zeros_like(kl[0])
    tz = jnp.stack([jnp.concatenate([kl[t - s] if t >= s else zero for t in range(ntok)], axis=-1)
                    for s in range(ntok)], axis=1)
    pair = lambda a: a.reshape(nb, ntok // 2, 2 * LANES, a.shape[-1])
    state_in = lambda w: pair(jnp.swapaxes(jnp.flip(w[:ntok], axis=0), 0, 1))
    state_out = lambda w: jnp.concatenate(list(w[:ntok]), axis=-1)
    apow = jnp.stack([cp['pw_re'][ntok], cp['pw_im'][ntok]], axis=1)
    return dict(tz=pair(tz).astype(BF16), wst_re=state_in(cp['w_re']), wst_im=state_in(cp['w_im']),
                wo_re=state_out(cp['wo_re']).astype(BF16), wo_im=state_out(cp['wo_im']).astype(BF16),
                apow=apow)


def _s5_body(u_ref, tz_ref, wre_ref, wim_ref, wore_ref, woim_ref, a_ref, x0re_ref, x0im_ref,
             y_ref, xre_ref, xim_ref, zre_scr, zim_scr, ub_scr,
             *, ntok, nchunks, nseq, npass):
    rows = nseq * nchunks
    nlb = (S5_GB * SSM_P) // LANES
    lane_blk = lambda lb: slice(lb * LANES, (lb + 1) * LANES)
    for sp in range(ntok // 2):
        us = jnp.concatenate([u_ref[pl.ds(2 * sp + d, rows, stride=ntok), :] for d in range(2)],
                             axis=1)
        uh, ul = _split_bf16(us)
        ub_scr[sp] = uh
        for w_ref, z_scr in ((wre_ref, zre_scr), (wim_ref, zim_scr)):
            wh, wl = _split_bf16(w_ref[sp])
            z = _dot(uh, wh)
            if npass >= 2:
                z = z + _dot(ul, wh)
            if npass >= 3:
                z = z + _dot(uh, wl)
            for lb in range(nlb):
                if sp == 0:
                    z_scr[lb] = z[:, lane_blk(lb)]
                else:
                    z_scr[lb] += z[:, lane_blk(lb)]
    a_re = a_ref[0:1, :]
    a_im = a_ref[1:2, :]

    def step(r, carry):
        x_re, x_im = carry
        at = pl.ds(r, nseq, stride=nchunks) if nchunks > 1 else pl.ds(0, nseq)
        z_re = jnp.concatenate([zre_scr[lb, at, :] for lb in range(nlb)], axis=1)
        z_im = jnp.concatenate([zim_scr[lb, at, :] for lb in range(nlb)], axis=1)
        for lb in range(nlb):
            zre_scr[lb, at, :] = x_re[:, lane_blk(lb)]
            zim_scr[lb, at, :] = x_im[:, lane_blk(lb)]
        return (a_re * x_re - a_im * x_im + z_re, a_re * x_im + a_im * x_re + z_im)

    x_re, x_im = lax.fori_loop(0, nchunks, step, (x0re_ref[0:nseq, :], x0im_ref[0:nseq, :]))
    xre_ref[...] = jnp.zeros_like(xre_ref)
    xim_ref[...] = jnp.zeros_like(xim_ref)
    xre_ref[0:nseq, :] = x_re
    xim_ref[0:nseq, :] = x_im
    for t0 in range(0, ntok, 2):
        cols = slice(t0 * LANES, (t0 + 2) * LANES)
        acc = jnp.zeros((rows, 2 * LANES), F32)
        for lb in range(0, nlb, 2):
            krows = slice(lb * LANES, (lb + 2) * LANES)
            for z_scr, wo_ref in ((zre_scr, wore_ref), (zim_scr, woim_ref)):
                xin = jnp.concatenate([z_scr[lb], z_scr[lb + 1]], axis=1).astype(BF16)
                acc = acc + _dot(xin, wo_ref[krows, cols])
        for sp in range(t0 // 2 + 1):
            acc = acc + _dot(ub_scr[sp], tz_ref[sp, :, cols])
        y_ref[pl.ds(t0, rows, stride=ntok), :] = acc[:, :LANES]
        y_ref[pl.ds(t0 + 1, rows, stride=ntok), :] = acc[:, LANES:]


def _s5(u, slot, ops, x0_re, x0_im, row0, nblk, nseq, nchunks, ntok, npass):
    nb = SSM_G // S5_GB
    rows = nseq * nchunks
    trows = rows * ntok
    assert row0 % trows == 0 and ntok % 2 == 0
    blk0 = row0 // trows
    nseq_pad = x0_re.shape[1]
    lanes = S5_GB * SSM_P
    if u.ndim == 3:
        u_spec = pl.BlockSpec((None, trows, LANES), lambda g, sb: (slot, blk0 + sb, g))
    else:
        u_spec = pl.BlockSpec((trows, LANES), lambda g, sb: (blk0 + sb, g))
    per_g = lambda shape: pl.BlockSpec((None,) + shape, lambda g, sb: (g,) + (0,) * len(shape))
    st_spec = pl.BlockSpec((None, nseq_pad, lanes), lambda g, sb: (sb, 0, g))
    st_shape = jax.ShapeDtypeStruct((nblk, nseq_pad, SSM_G * SSM_P), F32)
    return pl.pallas_call(
        functools.partial(_s5_body, ntok=ntok, nchunks=nchunks, nseq=nseq, npass=npass),
        grid=(nb, nblk),
        in_specs=[
            u_spec,
            per_g((ntok // 2, 2 * LANES, ntok * LANES)),
            per_g((ntok // 2, 2 * LANES, lanes)), per_g((ntok // 2, 2 * LANES, lanes)),
            per_g((lanes, ntok * LANES)), per_g((lanes, ntok * LANES)),
            per_g((2, lanes)),
            st_spec, st_spec,
        ],
        out_specs=[pl.BlockSpec((trows, LANES), lambda g, sb: (sb, g)), st_spec, st_spec],
        out_shape=[jax.ShapeDtypeStruct((nblk * trows, D_MODEL), F32), st_shape, st_shape],
        scratch_shapes=[pltpu.VMEM((lanes // LANES, rows, LANES), F32),
                        pltpu.VMEM((lanes // LANES, rows, LANES), F32),
                        pltpu.VMEM((ntok // 2, rows, 2 * LANES), BF16)],
        compiler_params=_params("arbitrary", "arbitrary"),
        name=f"s5_r{rows}_t{ntok}",
    )(u, ops['tz'], ops['wst_re'], ops['wst_im'], ops['wo_re'], ops['wo_im'], ops['apow'],
      x0_re, x0_im)


def _post_body(xa_ref, xb_ref, oa_ref, ob_ref, ya_ref, yb_ref, sg_ref, sa_ref, sb_ref, gnw_ref,
               wglu_ref, bglu_ref, wa_ref, wb_ref, wout_ref, out_ref, a_scr, *, n_first):
    first = pl.program_id(0) < n_first
    pick = lambda a_ref, b_ref: jnp.where(first, a_ref[...], b_ref[...])
    gnw = gnw_ref[...]
    o = pick(oa_ref, ob_ref)
    for h in range(HG_HEADS):
        hs = slice(h * HG_DIM, (h + 1) * HG_DIM)
        oh = o[:, hs]
        ms = jnp.mean(oh * oh, axis=-1, keepdims=True)
        a_scr[:, hs] = (oh * lax.rsqrt(ms + EPS) * gnw * sg_ref[:, hs]).astype(BF16)
    br_a = _dot(a_scr[...], wa_ref[...])
    y = _gelu_exact(pick(ya_ref, yb_ref))
    y = y * _sigmoid(_dot(y.astype(BF16), wglu_ref[...]) + bglu_ref[...])
    br_b = _dot(y.astype(BF16), wb_ref[...])
    mixed = sa_ref[...] * br_a + sb_ref[...] * br_b
    out_ref[...] = pick(xa_ref, xb_ref) + _dot(mixed.astype(BF16), wout_ref[...])


def _post(xa, xb, oa, ob, ya, yb, proj, g_norm_w, w_glu, b_glu, w_a, w_b, w_out, tm):
    t = xa.shape[0] + xb.shape[0]
    n_first = xa.shape[0] // tm
    assert xa.shape[0] % tm == 0 and xb.shape[0] % tm == 0 and proj.shape[1] == t
    spec_a, spec_b = _two_source_specs(tm, n_first)
    slot = lambda s: pl.BlockSpec((None, tm, D_MODEL), lambda i: (s, i, 0))
    full = lambda shape: pl.BlockSpec(shape, lambda i: (0,) * len(shape))
    sq = (D_MODEL, D_MODEL)
    return pl.pallas_call(
        functools.partial(_post_body, n_first=n_first),
        grid=(t // tm,),
        in_specs=[spec_a, spec_b, spec_a, spec_b, spec_a, spec_b, slot(3), slot(5), slot(6),
                  full((1, HG_DIM)), full(sq), full((1, D_MODEL)), full(sq), full(sq), full(sq)],
        out_specs=pl.BlockSpec((tm, D_MODEL), lambda i: (i, 0)),
        out_shape=jax.ShapeDtypeStruct((t, D_MODEL), F32),
        scratch_shapes=[pltpu.VMEM((tm, D_MODEL), BF16)],
        compiler_params=_params("arbitrary"),
        name="post",
    )(xa, xb, oa, ob, ya, yb, proj, proj, proj, g_norm_w.reshape(1, HG_DIM), w_glu.astype(BF16),
      b_glu.reshape(1, D_MODEL), w_a.astype(BF16), w_b.astype(BF16), w_out.astype(BF16))


def _sort_network(n):
    pairs = []
    p = 1
    while p < n:
        k = p
        while k >= 1:
            for j in range(k % p, n - k, 2 * k):
                for i in range(min(k, n - j - k)):
                    if (i + j) // (2 * p) == (i + j + k) // (2 * p):
                        pairs.append((i + j, i + j + k))
            k //= 2
        p *= 2
    return pairs


def _exchange(vs, i, j):
    vs[i], vs[j] = jnp.maximum(vs[i], vs[j]), jnp.minimum(vs[i], vs[j])


def _merge_top(a, b):
    n = len(a)
    vs = [jnp.maximum(a[i], b[n - 1 - i]) for i in range(n)]
    d = n // 2
    while d >= 1:
        for i in range(n):
            if i & d == 0:
                _exchange(vs, i, i + d)
        d //= 2
    return vs


def _merge_sublanes(vs):
    shift = SUBLANES // 2
    while shift >= 1:
        vs = _merge_top(vs, [pltpu.roll(v, shift, axis=0) for v in vs])
        shift //= 2
    return vs


def _top16_of_keys(s):
    vs = [s[v * SUBLANES:(v + 1) * SUBLANES] for v in range(PEER_NKEYS // SUBLANES)]
    for i, j in _sort_network(len(vs)):
        _exchange(vs, i, j)
    vs = _merge_sublanes(vs)
    return jnp.concatenate([v[0:1] for v in vs], axis=0)


_CAND_COUNTS = tuple(PEER_TOPK // (a + 1) for a in range(PEER_TOPK))


def _candidate_thresholds(v1, v2):
    sub = lax.broadcasted_iota(jnp.int32, (SUBLANES, LANES), 0)
    halves = []
    for half in range(PEER_TOPK // SUBLANES):
        v2h = v2[half * SUBLANES:(half + 1) * SUBLANES]
        halves.append([jnp.where(sub + half * SUBLANES < cnt, v1[a:a + 1] + v2h, -jnp.inf)
                       for a, cnt in enumerate(_CAND_COUNTS)])
    top = _merge_sublanes(_merge_top(halves[0], halves[1]))
    c16 = top[PEER_TOPK - 1][0:1]
    cands = halves[0] + halves[1]
    below = functools.reduce(jnp.maximum, [jnp.where(c < c16, c, -jnp.inf) for c in cands])
    c17 = jnp.max(below, axis=0, keepdims=True)
    return cands, c16, c17


def _peer_body(x_ref, n2_ref, fn_ref, wq_ref, k1_ref, k2_ref, u_ref, vt_ref, ya_ref, yb_ref,
               hnt_scr, s_scr, p_scr, tau_scr, vs_scr, act0, act1, g0, g1, acc_scr,
               *, tm, eb, n_first, nblk):
    i = pl.program_id(0)
    j = pl.program_id(1)
    nlt = tm // LANES
    nrow = eb // PEER_NKEYS
    assert nrow == SUBLANES

    @pl.when(j == 0)
    def _():
        acc_scr[...] = jnp.zeros_like(acc_scr)
        x = x_ref[...]
        ms = jnp.mean(x * x, axis=-1, keepdims=True)
        hn32 = x * lax.rsqrt(ms + EPS) * n2_ref[...]
        hnt_scr[...] = hn32.T.astype(BF16)
        qall = _dot(hn32.astype(BF16), wq_ref[...]).astype(BF16)
        for h in range(PEER_HEADS):
            for half, k_ref in enumerate((k1_ref, k2_ref)):
                c0 = (2 * h + half) * PEER_HALF
                sc = _dot_nt(k_ref[h], qall[:, c0:c0 + PEER_HALF])
                for lt in range(nlt):
                    s_scr[2 * h + half, lt] = sc[:, lt * LANES:(lt + 1) * LANES]

        def top_body(idx, carry):
            hh = idx // nlt
            lt = idx % nlt
            vs_scr[hh, lt] = _top16_of_keys(s_scr[hh, lt])
            return carry

        lax.fori_loop(0, 2 * PEER_HEADS * nlt, top_body, 0)

        def gate_body(idx, carry):
            h = idx // nlt
            lt = idx % nlt
            v1 = vs_scr[2 * h, lt]
            v2 = vs_scr[2 * h + 1, lt]
            cands, c16, c17 = _candidate_thresholds(v1, v2)
            top = v1[0:1] + v2[0:1]
            z = functools.reduce(jnp.add, [jnp.where(c >= c16, jnp.exp(c - top), 0.0) for c in cands])
            z = jnp.sum(z, axis=0, keepdims=True)
            tau_scr[h, lt] = jnp.broadcast_to(0.5 * (c16 + c17), (SUBLANES, LANES))
            s1 = s_scr[2 * h, lt]
            s2 = s_scr[2 * h + 1, lt]
            p_scr[2 * h, lt] = jnp.where(s1 >= v1[PEER_TOPK - 1:], jnp.exp(s1 - v1[0:1]), 0.0) / z
            p_scr[2 * h + 1, lt] = jnp.where(s2 >= v2[PEER_TOPK - 1:], jnp.exp(s2 - v2[0:1]), 0.0)
            return carry

        lax.fori_loop(0, PEER_HEADS * nlt, gate_body, 0)

    e1_rows = pl.ds(pl.multiple_of(jnp.clip(j - 1, 0, nblk - 1) * nrow, nrow), nrow)
    half_keys = PEER_NKEYS // 2

    def stages(act_new, act_cur, g_new, g_cur, first=True, middle=True, final=True):
        def pre_activation(rows):
            act_new[rows, :] = _dot(u_ref[rows, :], hnt_scr[...])

        def accumulate(rows):
            acc_scr[rows, :] += _dot(vt_ref[rows, :], g_cur[...])

        def gated(lt, r0, kh):
            cols = slice(lt * LANES, (lt + 1) * LANES)
            keys = slice(kh * half_keys, (kh + 1) * half_keys)
            ws = [jnp.zeros((half_keys, LANES), F32) for _ in range(2)]
            for h in range(PEER_HEADS):
                s2 = s_scr[2 * h + 1, lt, keys, :]
                p2 = p_scr[2 * h + 1, lt, keys, :]
                s1grp = s_scr[2 * h, lt, e1_rows, :]
                p1grp = p_scr[2 * h, lt, e1_rows, :]
                tau = tau_scr[h, lt][0:1]
                for d in range(2):
                    theta = tau - s1grp[r0 + d:r0 + d + 1]
                    ws[d] = ws[d] + jnp.where(s2 >= theta, p2, 0.0) * p1grp[r0 + d:r0 + d + 1]
            for d in range(2):
                rows = slice((r0 + d) * PEER_NKEYS + kh * half_keys, (r0 + d) * PEER_NKEYS + (kh + 1) * half_keys)
                g_new[rows, cols] = (ws[d] * _gelu_exact(act_cur[rows, cols])).astype(BF16)

        chunk = 4 * LANES
        matmul_jobs = []
        if first:
            matmul_jobs += [functools.partial(pre_activation, slice(m, m + chunk)) for m in range(0, eb, chunk)]
        if final:
            matmul_jobs += [functools.partial(accumulate, slice(m, m + chunk)) for m in range(0, D_MODEL, chunk)]
        gated_jobs = []
        if middle:
            gated_jobs = [functools.partial(gated, lt, r0, kh)
                          for lt in range(nlt) for r0 in range(0, nrow, 2) for kh in range(2)]
        per = -(-len(gated_jobs) // len(matmul_jobs))
        for k, job in enumerate(matmul_jobs):
            job()
            for gjob in gated_jobs[k * per:(k + 1) * per]:
                gjob()

    nstep = nblk + 2
    assert nblk % 2 == 0 and nblk >= 4
    steady = (j >= 2) & (j < nblk)

    @pl.when(j == 0)
    def _():
        stages(act0, act1, g0, g1, middle=False, final=False)

    @pl.when(j == 1)
    def _():
        stages(act1, act0, g1, g0, final=False)

    @pl.when(steady & (j % 2 == 0))
    def _():
        stages(act0, act1, g0, g1)

    @pl.when(steady & (j % 2 == 1))
    def _():
        stages(act1, act0, g1, g0)

    @pl.when(j == nstep - 2)
    def _():
        stages(act0, act1, g0, g1, first=False)

    @pl.when(j == nstep - 1)
    def _():
        stages(act1, act0, g1, g0, first=False, middle=False)

    def finish(y_ref):
        xo = x_ref[...] + acc_scr[...].T
        ms = jnp.mean(xo * xo, axis=-1, keepdims=True)
        y_ref[...] = xo * lax.rsqrt(ms + EPS) * fn_ref[...]

    last = j == pl.num_programs(1) - 1

    @pl.when(last & (i < n_first))
    def _():
        finish(ya_ref)

    @pl.when(last & (i >= n_first))
    def _():
        finish(yb_ref)


def _peer(x, n_first_rows, norm2_w, final_norm_w, wq, k1, k2, u_tab, v_tab, tm, eb):
    t = x.shape[0]
    assert t % tm == 0 and n_first_rows % tm == 0
    n_first = n_first_rows // tm
    nexp = u_tab.shape[0]
    assert nexp == PEER_NKEYS * PEER_NKEYS and nexp % eb == 0
    nblk = nexp // eb
    qw = 2 * PEER_HEADS * PEER_HALF
    nlt = tm // LANES
    full = lambda shape: pl.BlockSpec(shape, lambda i, j: (0,) * len(shape))
    out_a, out_b = _two_source_specs(tm, n_first)
    return pl.pallas_call(
        functools.partial(_peer_body, tm=tm, eb=eb, n_first=n_first, nblk=nblk),
        grid=(t // tm, nblk + 2),
        in_specs=[
            pl.BlockSpec((tm, D_MODEL), lambda i, j: (i, 0)),
            full((1, D_MODEL)), full((1, D_MODEL)),
            full((D_MODEL, qw)),
            full((PEER_HEADS, PEER_NKEYS, PEER_HALF)), full((PEER_HEADS, PEER_NKEYS, PEER_HALF)),
            pl.BlockSpec((eb, D_MODEL), lambda i, j: (jnp.minimum(j, nblk - 1), 0)),
            pl.BlockSpec((None, D_MODEL, eb), lambda i, j: (jnp.clip(j - 2, 0, nblk - 1), 0, 0)),
        ],
        out_specs=[out_a, out_b],
        out_shape=[jax.ShapeDtypeStruct((n_first_rows, D_MODEL), F32),
                   jax.ShapeDtypeStruct((t - n_first_rows, D_MODEL), F32)],
        scratch_shapes=[
            pltpu.VMEM((D_MODEL, tm), BF16),
            pltpu.VMEM((2 * PEER_HEADS, nlt, PEER_NKEYS, LANES), F32),
            pltpu.VMEM((2 * PEER_HEADS, nlt, PEER_NKEYS, LANES), F32),
            pltpu.VMEM((PEER_HEADS, nlt, SUBLANES, LANES), F32),
            pltpu.VMEM((2 * PEER_HEADS, nlt, PEER_TOPK, LANES), F32),
            pltpu.VMEM((eb, tm), F32), pltpu.VMEM((eb, tm), F32),
            pltpu.VMEM((eb, tm), BF16), pltpu.VMEM((eb, tm), BF16),
            pltpu.VMEM((D_MODEL, tm), F32),
        ],
        compiler_params=_params("arbitrary", "arbitrary"),
        name="peer",
    )(x, norm2_w.reshape(1, D_MODEL), final_norm_w.reshape(1, D_MODEL), wq.astype(BF16),
      k1.astype(BF16), k2.astype(BF16), u_tab.astype(BF16),
      jnp.transpose(v_tab.astype(BF16).reshape(nblk, eb, D_MODEL), (0, 2, 1)))


def kernel(x_prompt, x_sample, state_hgrn, state_ssm_re, state_ssm_im, meta_tokens, lower_bounds,
           norm1_w, w_in, g_norm_w, ssm_a_re, ssm_a_im, ssm_log_step, ssm_b_re, ssm_b_im,
           ssm_c_re, ssm_c_im, ssm_d, w_glu, b_glu, w_branch_a, w_branch_b, w_out, norm2_w,
           peer_wq, peer_k1, peer_k2, peer_u, peer_v, final_norm_w):
    depth = w_in.shape[0]
    assert depth == 1, "single-layer step only"
    nb, seq, _ = x_prompt.shape
    ns, dseq, _ = x_sample.shape
    tp = nb * seq
    tsm = ns * dseq
    tm = 512
    tm_in = 256
    hg_c = 128
    small_c = 16
    s5_tok = 8
    s5_seq = 4
    assert tp % tm == 0 and tsm % tm == 0 and tm % tm_in == 0 and seq % hg_c == 0 and seq % s5_tok == 0
    assert nb % s5_seq == 0 and dseq <= small_c and dseq % 2 == 0 and N_META % s5_tok == 0

    w_in_b = w_in[0].astype(BF16)
    xp = x_prompt.reshape(tp, D_MODEL).astype(F32)
    xs = x_sample.reshape(tsm, D_MODEL).astype(F32)
    proj, lg = _inproj(xp, xs, norm1_w[0], lower_bounds, w_in_b, tm_in)
    proj_m, lg_m = _inproj(meta_tokens.astype(F32), None, norm1_w[0], lower_bounds, w_in_b, N_META)

    zero_hg = jnp.zeros((1, HG_HEADS, HG_DIM, HG_DIM), F32)
    _, hg_meta = _hgrn(proj_m, lg_m, zero_hg, 1, 1, N_META, 0)
    o_p, hg_p = _hgrn(proj, lg, hg_meta, nb, seq // hg_c, hg_c, 0)
    o_s, hg_s = _hgrn(proj, lg, state_hgrn[0].astype(F32), ns, 1, small_c, tp, ntok=dseq)
    o_s = o_s.reshape(tsm, D_MODEL)

    ssm = (ssm_a_re[0], ssm_a_im[0], ssm_log_step[0], ssm_b_re[0], ssm_b_im[0], ssm_c_re[0],
           ssm_c_im[0], ssm_d[0])
    compact = _s5_pieces(*ssm, max(s5_tok, dseq))
    ops_p = _s5_operators(compact, s5_tok)
    ops_s = _s5_operators(compact, dseq)
    nst = SSM_G * SSM_P
    u_meta = jnp.pad(proj_m[4], ((0, 7 * N_META), (0, 0)))
    zero_ss = jnp.zeros((1, SUBLANES, nst), F32)
    _, mre, mim = _s5(u_meta, 0, ops_p, zero_ss, zero_ss, 0, 1, SUBLANES, N_META // s5_tok, s5_tok, 3)
    nblk = nb // s5_seq
    x0re = jnp.broadcast_to(mre[:, 0:1], (nblk, SUBLANES, nst))
    x0im = jnp.broadcast_to(mim[:, 0:1], (nblk, SUBLANES, nst))
    y_p, pre, pim = _s5(proj, 4, ops_p, x0re, x0im, 0, nblk, s5_seq, seq // s5_tok, s5_tok, 1)
    y_s, sre, sim = _s5(proj, 4, ops_s, state_ssm_re[0].astype(F32).reshape(1, ns, nst),
                        state_ssm_im[0].astype(F32).reshape(1, ns, nst), tp, 1, ns, 1, dseq, 3)

    x1 = _post(xp, xs, o_p, o_s, y_p, y_s, proj, g_norm_w[0], w_glu[0], b_glu[0], w_branch_a[0],
               w_branch_b[0], w_out[0], tm)
    y_pr, y_sm = _peer(x1, tp, norm2_w[0], final_norm_w, peer_wq[0], peer_k1[0], peer_k2[0],
                       peer_u[0], peer_v[0], tm, SUBLANES * PEER_NKEYS)

    sd = state_hgrn.dtype
    st = lambda a: a[:, :s5_seq].reshape(nb, SSM_G, SSM_P)[None]
    return (y_pr.reshape(nb, seq, D_MODEL).astype(x_prompt.dtype),
            y_sm.reshape(ns, dseq, D_MODEL).astype(x_sample.dtype),
            hg_p[None].astype(sd),
            st(pre).astype(state_ssm_re.dtype),
            st(pim).astype(state_ssm_im.dtype),
            hg_s[None].astype(sd),
            sre.reshape(1, ns, SSM_G, SSM_P).astype(state_ssm_re.dtype),
            sim.reshape(1, ns, SSM_G, SSM_P).astype(state_ssm_im.dtype))
```

```python
import functools
import math

import numpy as np
import jax
import jax.numpy as jnp
from jax import lax
from jax.experimental import pallas as pl
from jax.experimental.pallas import tpu as pltpu

F32 = jnp.float32
BF16 = jnp.bfloat16

D_MODEL = 1024
N_META = 16
HG_HEADS = 8
HG_DIM = 128
SSM_G = 64
SSM_GROUP = 16
SSM_P = 64
SSM_MIN_RE = -1e-4
S5_GB = 8
PEER_HEADS = 8
PEER_NKEYS = 128
PEER_TOPK = 16
PEER_HALF = 128
EPS = 1e-6
LANES = 128
SUBLANES = 8
VMEM_LIMIT = 56 * 1024 * 1024


def _dot(a, b):
    return jnp.dot(a, b, preferred_element_type=F32)


def _dot_nt(a, b):
    return lax.dot_general(a, b, (((1,), (1,)), ((), ())), preferred_element_type=F32)


def _dot_tn(a, b):
    return lax.dot_general(a, b, (((0,), (0,)), ((), ())), preferred_element_type=F32)


def _split_bf16(x):
    hi = x.astype(BF16)
    lo = (x - hi.astype(F32)).astype(BF16)
    return hi, lo


def _sigmoid(x):
    return 1.0 / (1.0 + jnp.exp(-x))


def _gelu_exact(x):
    return 0.5 * x * (1.0 + lax.erf(x * (1.0 / math.sqrt(2.0))))


def _params(*sem):
    return pltpu.CompilerParams(dimension_semantics=sem, vmem_limit_bytes=VMEM_LIMIT)


def _two_source_specs(tm, n_first, ncols=D_MODEL):
    first = pl.BlockSpec((tm, ncols), lambda i, *_: (jnp.minimum(i, n_first - 1), 0))
    second = pl.BlockSpec((tm, ncols), lambda i, *_: (jnp.maximum(i - n_first, 0), 0))
    return first, second


N_SEG = 7


def _inproj_body(xa_ref, xb_ref, nw_ref, lbs_ref, w_ref, o_ref, lg_ref, *, n_first):
    x = jnp.where(pl.program_id(0) < n_first, xa_ref[...], xb_ref[...])
    ms = jnp.mean(x * x, axis=-1, keepdims=True)
    h = (x * lax.rsqrt(ms + EPS) * nw_ref[...]).astype(BF16)
    lbs = lbs_ref[...]
    e = jnp.exp(lbs - jnp.max(lbs, axis=0, keepdims=True))
    lb = e[0:1] / jnp.sum(e, axis=0, keepdims=True)
    for seg in range(N_SEG):
        p = _dot(h, w_ref[:, seg * D_MODEL:(seg + 1) * D_MODEL])
        if seg in (0, 3):
            o_ref[seg] = p * _sigmoid(p)
        elif seg == 1:
            fg = lb + (1.0 - lb) * _sigmoid(p)
            o_ref[seg] = 1.0 - fg
            lg_ref[...] = jnp.log(fg)
        elif seg in (2, 4):
            o_ref[seg] = p
        else:
            o_ref[seg] = _sigmoid(p)


def _inproj(xa, xb, norm_w, lower_bounds, w_in_bf16, tm):
    if xb is None:
        xb = xa
        t = xa.shape[0]
        n_first = t // tm
    else:
        t = xa.shape[0] + xb.shape[0]
        n_first = xa.shape[0] // tm
        assert xa.shape[0] % tm == 0 and xb.shape[0] % tm == 0
    spec_a, spec_b = _two_source_specs(tm, n_first)
    return pl.pallas_call(
        functools.partial(_inproj_body, n_first=n_first),
        grid=(t // tm,),
        in_specs=[
            spec_a, spec_b,
            pl.BlockSpec((1, D_MODEL), lambda i: (0, 0)),
            pl.BlockSpec(lower_bounds.shape, lambda i: (0, 0)),
            pl.BlockSpec((D_MODEL, N_SEG * D_MODEL), lambda i: (0, 0)),
        ],
        out_specs=[
            pl.BlockSpec((N_SEG, tm, D_MODEL), lambda i: (0, i, 0)),
            pl.BlockSpec((tm, D_MODEL), lambda i: (i, 0)),
        ],
        out_shape=[
            jax.ShapeDtypeStruct((N_SEG, t, D_MODEL), F32),
            jax.ShapeDtypeStruct((t, D_MODEL), F32),
        ],
        compiler_params=_params("arbitrary"),
        name="inproj",
    )(xa, xb, norm_w.reshape(1, D_MODEL), lower_bounds, w_in_bf16)


def _hgrn_consts(c):
    nlev = int(round(math.log2(c)))
    assert 1 << nlev == c and c % SUBLANES == 0
    tri = np.tril(np.ones((c, c), np.float32))
    lev = np.full((c, c), -1, np.int32)
    for t in range(c):
        for s in range(t + 1):
            lev[t, s] = (t ^ s).bit_length()
    return tri, lev, nlev


def _decay_factors(lg, tri, c, nlev):
    hi, lo = _split_bf16(lg)
    b = _dot(tri, hi) + _dot(tri, lo)
    row = lambda r: jnp.broadcast_to(b[r:r + 1], (SUBLANES, HG_DIM))
    sub = lax.broadcasted_iota(jnp.int32, (SUBLANES, HG_DIM), 0)
    tidx = lax.broadcasted_iota(jnp.int32, (c, HG_DIM), 0)
    out = [jnp.exp(b), jnp.exp(b[c - 1:c] - b)]
    for l in range(1, nlev + 1):
        blk = 1 << l
        half = blk >> 1
        upper = ((tidx >> (l - 1)) & 1) == 1
        if l == 1:
            e = jnp.where(upper, lg, 0.0)
        else:
            groups = []
            for t0 in range(0, c, SUBLANES):
                if blk >= SUBLANES:
                    groups.append(row((t0 // blk) * blk + half - 1))
                else:
                    groups.append(jnp.where(sub < blk, row(t0 + half - 1), row(t0 + blk + half - 1)))
            bref = jnp.concatenate(groups, axis=0)
            e = jnp.where(upper, b - bref, bref - b)
        out.append(jnp.exp(e))
    return out, hi, lo


def _hgrn_body(q_ref, k_ref, v_ref, lg_ref, s0_ref, d_ref, lev_ref, o_ref, sout_ref, s_scr,
               *pad_scr, c, nlev, ntok, nsub):
    n = pl.program_id(0)
    r = pl.program_id(1)

    @pl.when(r == 0)
    def _():
        for sub in range(nsub):
            s_scr[sub] = s0_ref[sub if s0_ref.shape[0] == nsub else 0]

    if ntok < c:
        (pad,) = pad_scr

        @pl.when((n == 0) & (r == 0))
        def _():
            pad[...] = jnp.zeros_like(pad)

    tri = d_ref[...]
    lev = lev_ref[...]
    ones = jnp.ones((c, HG_DIM), BF16)
    tidx = lax.broadcasted_iota(jnp.int32, (c, HG_DIM), 0)
    for sub in range(nsub):
        rows = slice(sub * ntok, (sub + 1) * ntok)
        if ntok < c:
            for a, ref in enumerate((q_ref, k_ref, v_ref, lg_ref)):
                pad[sub, a, 0:ntok, :] = ref[rows, :]
            qr, kr, vr, lr = (pad.at[sub, a] for a in range(4))
        else:
            qr, kr, vr, lr = q_ref, k_ref, v_ref, lg_ref
        heads = [slice(h * HG_DIM, (h + 1) * HG_DIM) for h in range(HG_HEADS)]
        facs = [_decay_factors(lr[:, hs], tri, c, nlev) for hs in heads]
        xs = [f[0] for f in facs]
        lgs = [(f[1], f[2]) for f in facs]
        atts = []
        for hs, x in zip(heads, xs):
            q = qr[:, hs]
            k = kr[:, hs]
            att = jnp.where(lev == 0, _dot_nt(q.astype(BF16), k.astype(BF16)), 0.0)
            for l in range(1, nlev + 1):
                upper = ((tidx >> (l - 1)) & 1) == 1
                m = (jnp.where(upper, q, k) * x[1 + l]).astype(BF16)
                att = jnp.where(lev == l, _dot_nt(m, m), att)
            atts.append(att.astype(BF16))
        for h, (hs, x, att) in enumerate(zip(heads, xs, atts)):
            qd = (qr[:, hs] * x[0]).astype(BF16)
            o = _dot(qd, s_scr[sub, h].astype(BF16)) + _dot(att, vr[:, hs].astype(BF16))
            o_ref[rows, hs] = o[0:ntok]
        for h, (hs, x, (hi, lo)) in enumerate(zip(heads, xs, lgs)):
            kd = (kr[:, hs] * x[1]).astype(BF16)
            gl = _dot_tn(hi, ones) + _dot_tn(lo, ones)
            s_scr[sub, h] = jnp.exp(gl) * s_scr[sub, h] + _dot_tn(kd, vr[:, hs].astype(BF16))

    @pl.when(r == pl.num_programs(1) - 1)
    def _():
        sout_ref[...] = s_scr[...]


def _hgrn(proj, lg, s0, nseq, nchunks, c, row0, ntok=None):
    ntok = c if ntok is None else ntok
    dnp, levnp, nlev = _hgrn_consts(c)
    assert ntok == c or nchunks == 1
    nsub = 1 if ntok == c else (2 * SUBLANES) // ntok
    step_rows = nsub * ntok
    assert row0 % step_rows == 0 and nseq % nsub == 0 and step_rows % SUBLANES == 0
    blk0 = row0 // step_rows
    tok_specs = [pl.BlockSpec((None, step_rows, D_MODEL),
                              functools.partial(lambda n, r, s: (s, blk0 + n * nchunks + r, 0), s=s))
                 for s in range(3)]
    lg_spec = pl.BlockSpec((step_rows, D_MODEL), lambda n, r: (blk0 + n * nchunks + r, 0))
    o_spec = pl.BlockSpec((step_rows, D_MODEL), lambda n, r: (n * nchunks + r, 0))
    o_shape = jax.ShapeDtypeStruct((nseq * nchunks * ntok, D_MODEL), F32)
    scratch = [] if ntok == c else [pltpu.VMEM((nsub, 4, c, D_MODEL), F32)]
    st_blk = (nsub, HG_HEADS, HG_DIM, HG_DIM)
    if s0.shape[0] == 1:
        s0_spec = pl.BlockSpec((1,) + st_blk[1:], lambda n, r: (0, 0, 0, 0))
    else:
        s0_spec = pl.BlockSpec(st_blk, lambda n, r: (n, 0, 0, 0))
    return pl.pallas_call(
        functools.partial(_hgrn_body, c=c, nlev=nlev, ntok=ntok, nsub=nsub),
        grid=(nseq // nsub, nchunks),
        in_specs=tok_specs + [
            lg_spec, s0_spec,
            pl.BlockSpec(dnp.shape, lambda n, r: (0, 0)),
            pl.BlockSpec(levnp.shape, lambda n, r: (0, 0)),
        ],
        out_specs=[o_spec, pl.BlockSpec(st_blk, lambda n, r: (n, 0, 0, 0))],
        out_shape=[o_shape, jax.ShapeDtypeStruct((nseq, HG_HEADS, HG_DIM, HG_DIM), F32)],
        scratch_shapes=[pltpu.VMEM(st_blk, F32)] + scratch,
        compiler_params=_params("arbitrary", "arbitrary"),
        name=f"hgrn_c{c}_t{ntok}",
    )(proj, proj, proj, lg, s0, jnp.asarray(dnp, BF16), jnp.asarray(levnp))


def _s5_pieces_body(are_ref, aim_ref, ls_ref, btre_ref, btim_ref, ccre_ref, ccim_ref, d_ref,
                     kl_ref, wre_ref, wim_ref, wore_ref, woim_ref, pwre_ref, pwim_ref, *, nmax):
    hp = lax.Precision.HIGHEST
    rows = S5_GB * SSM_GROUP
    lanes = S5_GB * SSM_P
    lam_re = jnp.minimum(are_ref[...], SSM_MIN_RE)
    lam_im = aim_ref[...]
    delta = jnp.exp(ls_ref[...])
    pw_re, pw_im = [], []
    for n in range(nmax + 1):
        mag = jnp.exp(lam_re * delta * float(n))
        ang = lam_im * delta * float(n)
        pw_re.append(mag * jnp.cos(ang))
        pw_im.append(mag * jnp.sin(ang))
        pwre_ref[n] = pw_re[n]
        pwim_ref[n] = pw_im[n]
    x = pw_re[1] - 1.0
    y = pw_im[1]
    den = lam_re * lam_re + lam_im * lam_im
    cr = (x * lam_re + y * lam_im) / den
    ci = (y * lam_re - x * lam_im) / den
    bb_re = cr * btre_ref[...] - ci * btim_ref[...]
    bb_im = cr * btim_ref[...] + ci * btre_ref[...]
    cc_re = ccre_ref[...]
    cc_im = ccim_ref[...]
    row_g = lax.broadcasted_iota(jnp.int32, (rows, rows), 0) // SSM_GROUP
    same_c = row_g == lax.broadcasted_iota(jnp.int32, (rows, rows), 1) // SSM_GROUP
    eye = (lax.broadcasted_iota(jnp.int32, (rows, rows), 0)
           == lax.broadcasted_iota(jnp.int32, (rows, rows), 1)).astype(F32)
    same_p = (lax.broadcasted_iota(jnp.int32, (rows, lanes), 0) // SSM_GROUP
              == lax.broadcasted_iota(jnp.int32, (rows, lanes), 1) // SSM_P)
    rep = (lax.broadcasted_iota(jnp.int32, (SSM_P, lanes), 0)
           == lax.broadcasted_iota(jnp.int32, (SSM_P, lanes), 1) % SSM_P).astype(F32)
    def exact3(t):
        t1 = t.astype(BF16)
        r1 = t - t1.astype(F32)
        t2 = r1.astype(BF16)
        return t1, t2, (r1 - t2.astype(F32)).astype(BF16)

    rep_b = rep.astype(BF16)
    eye_b = eye.astype(BF16)
    spread = lambda t: jnp.where(same_p, sum(_dot(part, rep_b) for part in exact3(t)), 0.0)
    nt = lambda u, v: lax.dot_general(u, v, (((1,), (1,)), ((), ())), precision=hp)
    tn = lambda u, v: sum(_dot_tn(part, v) for part in exact3(u))
    for n in range(nmax):
        w_re = pw_re[n] * bb_re - pw_im[n] * bb_im
        w_im = pw_re[n] * bb_im + pw_im[n] * bb_re
        kl = jnp.where(same_c, nt(w_re, cc_re) - nt(w_im, cc_im), 0.0)
        if n == 0:
            kl = kl + d_ref[...] * eye
        kl_ref[n] = kl
        wre_ref[n] = spread(w_re)
        wim_ref[n] = spread(w_im)
        o_re = cc_re * pw_re[n + 1] - cc_im * pw_im[n + 1]
        o_im = -(cc_re * pw_im[n + 1] + cc_im * pw_re[n + 1])
        wore_ref[n] = tn(spread(o_re), eye_b)
        woim_ref[n] = tn(spread(o_im), eye_b)


def _s5_pieces(a_re, a_im, log_step, b_re, b_im, c_re, c_im, d_skip, nmax):
    nb = SSM_G // S5_GB
    rows = S5_GB * SSM_GROUP
    lanes = S5_GB * SSM_P
    per_row = lambda a: jnp.repeat(a.astype(F32), SSM_GROUP, axis=0)
    flat = lambda a: a.astype(F32).reshape(SSM_G * SSM_GROUP, SSM_P)
    tile = pl.BlockSpec((rows, SSM_P), lambda g: (g, 0))
    out4 = lambda r, c: pl.BlockSpec((nmax, None, r, c), lambda g: (0, g, 0, 0))
    shape4 = lambda r, c: jax.ShapeDtypeStruct((nmax, nb, r, c), F32)
    pw_spec = pl.BlockSpec((nmax + 1, rows, SSM_P), lambda g: (0, g, 0))
    pw_shape = jax.ShapeDtypeStruct((nmax + 1, SSM_G * SSM_GROUP, SSM_P), F32)
    kl, w_re, w_im, wo_re, wo_im, pw_re, pw_im = pl.pallas_call(
        functools.partial(_s5_pieces_body, nmax=nmax),
        grid=(nb,),
        in_specs=[tile] * 7 + [pl.BlockSpec((rows, rows), lambda g: (g, 0))],
        out_specs=[out4(rows, rows), out4(rows, lanes), out4(rows, lanes), out4(lanes, rows),
                   out4(lanes, rows), pw_spec, pw_spec],
        out_shape=[shape4(rows, rows), shape4(rows, lanes), shape4(rows, lanes), shape4(lanes, rows),
                   shape4(lanes, rows), pw_shape, pw_shape],
        compiler_params=_params("arbitrary"),
        name="s5_pieces",
    )(per_row(a_re), per_row(a_im),
      per_row(jnp.broadcast_to(log_step.astype(F32)[:, None], (SSM_G, SSM_P))),
      flat(jnp.swapaxes(b_re, 1, 2)), flat(jnp.swapaxes(b_im, 1, 2)), flat(c_re), flat(c_im),
      jnp.broadcast_to(d_skip.astype(F32).reshape(SSM_G * SSM_GROUP, 1), (SSM_G * SSM_GROUP, rows)))
    per_group = lambda p: p[:, ::SSM_GROUP, :].reshape(nmax + 1, nb, lanes)
    return dict(kl=kl, w_re=w_re, w_im=w_im, wo_re=wo_re, wo_im=wo_im,
                pw_re=per_group(pw_re), pw_im=per_group(pw_im))


def _s5_operators(cp, ntok):
    nb = SSM_G // S5_GB
    kl = cp['kl']
    zero = jnp.zeros_like(kl[0])
    tz = jnp.stack([jnp.concatenate([kl[t - s] if t >= s else zero for t in range(ntok)], axis=-1)
                    for s in range(ntok)], axis=1)
    pair = lambda a: a.reshape(nb, ntok // 2, 2 * LANES, a.shape[-1])
    state_in = lambda w: pair(jnp.swapaxes(jnp.flip(w[:ntok], axis=0), 0, 1))
    state_out = lambda w: jnp.concatenate(list(w[:ntok]), axis=-1)
    apow = jnp.stack([cp['pw_re'][ntok], cp['pw_im'][ntok]], axis=1)
    return dict(tz=pair(tz).astype(BF16), wst_re=state_in(cp['w_re']), wst_im=state_in(cp['w_im']),
                wo_re=state_out(cp['wo_re']).astype(BF16), wo_im=state_out(cp['wo_im']).astype(BF16),
                apow=apow)


def _s5_body(u_ref, tz_ref, wre_ref, wim_ref, wore_ref, woim_ref, a_ref, x0re_ref, x0im_ref,
             y_ref, xre_ref, xim_ref, zre_scr, zim_scr, ub_scr,
             *, ntok, nchunks, nseq, npass):
    rows = nseq * nchunks
    nlb = (S5_GB * SSM_P) // LANES
    lane_blk = lambda lb: slice(lb * LANES, (lb + 1) * LANES)
    for sp in range(ntok // 2):
        us = jnp.concatenate([u_ref[pl.ds(2 * sp + d, rows, stride=ntok), :] for d in range(2)],
                             axis=1)
        uh, ul = _split_bf16(us)
        ub_scr[sp] = uh
        for w_ref, z_scr in ((wre_ref, zre_scr), (wim_ref, zim_scr)):
            wh, wl = _split_bf16(w_ref[sp])
            z = _dot(uh, wh)
            if npass >= 2:
                z = z + _dot(ul, wh)
            if npass >= 3:
                z = z + _dot(uh, wl)
            for lb in range(nlb):
                if sp == 0:
                    z_scr[lb] = z[:, lane_blk(lb)]
                else:
                    z_scr[lb] += z[:, lane_blk(lb)]
    a_re = a_ref[0:1, :]
    a_im = a_ref[1:2, :]

    def step(r, carry):
        x_re, x_im = carry
        at = pl.ds(r, nseq, stride=nchunks) if nchunks > 1 else pl.ds(0, nseq)
        z_re = jnp.concatenate([zre_scr[lb, at, :] for lb in range(nlb)], axis=1)
        z_im = jnp.concatenate([zim_scr[lb, at, :] for lb in range(nlb)], axis=1)
        for lb in range(nlb):
            zre_scr[lb, at, :] = x_re[:, lane_blk(lb)]
            zim_scr[lb, at, :] = x_im[:, lane_blk(lb)]
        return (a_re * x_re - a_im * x_im + z_re, a_re * x_im + a_im * x_re + z_im)

    x_re, x_im = lax.fori_loop(0, nchunks, step, (x0re_ref[0:nseq, :], x0im_ref[0:nseq, :]),
                               unroll=2 if nchunks % 2 == 0 else 1)
    xre_ref[...] = jnp.zeros_like(xre_ref)
    xim_ref[...] = jnp.zeros_like(xim_ref)
    xre_ref[0:nseq, :] = x_re
    xim_ref[0:nseq, :] = x_im
    for t0 in range(0, ntok, 2):
        cols = slice(t0 * LANES, (t0 + 2) * LANES)
        acc = jnp.zeros((rows, 2 * LANES), F32)
        for lb in range(0, nlb, 2):
            krows = slice(lb * LANES, (lb + 2) * LANES)
            for z_scr, wo_ref in ((zre_scr, wore_ref), (zim_scr, woim_ref)):
                xin = jnp.concatenate([z_scr[lb], z_scr[lb + 1]], axis=1).astype(BF16)
                acc = acc + _dot(xin, wo_ref[krows, cols])
        for sp in range(t0 // 2 + 1):
            acc = acc + _dot(ub_scr[sp], tz_ref[sp, :, cols])
        y_ref[pl.ds(t0, rows, stride=ntok), :] = acc[:, :LANES]
        y_ref[pl.ds(t0 + 1, rows, stride=ntok), :] = acc[:, LANES:]


def _s5(u, slot, ops, x0_re, x0_im, row0, nblk, nseq, nchunks, ntok, npass):
    nb = SSM_G // S5_GB
    rows = nseq * nchunks
    trows = rows * ntok
    assert row0 % trows == 0 and ntok % 2 == 0
    blk0 = row0 // trows
    nseq_pad = x0_re.shape[1]
    lanes = S5_GB * SSM_P
    if u.ndim == 3:
        u_spec = pl.BlockSpec((None, trows, LANES), lambda g, sb: (slot, blk0 + sb, g))
    else:
        u_spec = pl.BlockSpec((trows, LANES), lambda g, sb: (blk0 + sb, g))
    per_g = lambda shape: pl.BlockSpec((None,) + shape, lambda g, sb: (g,) + (0,) * len(shape))
    st_spec = pl.BlockSpec((None, nseq_pad, lanes), lambda g, sb: (sb, 0, g))
    st_shape = jax.ShapeDtypeStruct((nblk, nseq_pad, SSM_G * SSM_P), F32)
    return pl.pallas_call(
        functools.partial(_s5_body, ntok=ntok, nchunks=nchunks, nseq=nseq, npass=npass),
        grid=(nb, nblk),
        in_specs=[
            u_spec,
            per_g((ntok // 2, 2 * LANES, ntok * LANES)),
            per_g((ntok // 2, 2 * LANES, lanes)), per_g((ntok // 2, 2 * LANES, lanes)),
            per_g((lanes, ntok * LANES)), per_g((lanes, ntok * LANES)),
            per_g((2, lanes)),
            st_spec, st_spec,
        ],
        out_specs=[pl.BlockSpec((trows, LANES), lambda g, sb: (sb, g)), st_spec, st_spec],
        out_shape=[jax.ShapeDtypeStruct((nblk * trows, D_MODEL), F32), st_shape, st_shape],
        scratch_shapes=[pltpu.VMEM((lanes // LANES, rows, LANES), F32),
                        pltpu.VMEM((lanes // LANES, rows, LANES), F32),
                        pltpu.VMEM((ntok // 2, rows, 2 * LANES), BF16)],
        compiler_params=_params("arbitrary", "arbitrary"),
        name=f"s5_r{rows}_t{ntok}",
    )(u, ops['tz'], ops['wst_re'], ops['wst_im'], ops['wo_re'], ops['wo_im'], ops['apow'],
      x0_re, x0_im)


def _post_body(xa_ref, xb_ref, oa_ref, ob_ref, ya_ref, yb_ref, sg_ref, sa_ref, sb_ref, gnw_ref,
               wglu_ref, bglu_ref, wa_ref, wb_ref, wout_ref, out_ref, a_scr, *, n_first):
    first = pl.program_id(0) < n_first
    pick = lambda a_ref, b_ref: jnp.where(first, a_ref[...], b_ref[...])
    gnw = gnw_ref[...]
    o = pick(oa_ref, ob_ref)
    for h in range(HG_HEADS):
        hs = slice(h * HG_DIM, (h + 1) * HG_DIM)
        oh = o[:, hs]
        ms = jnp.mean(oh * oh, axis=-1, keepdims=True)
        a_scr[:, hs] = (oh * lax.rsqrt(ms + EPS) * gnw * sg_ref[:, hs]).astype(BF16)
    br_a = _dot(a_scr[...], wa_ref[...])
    y = _gelu_exact(pick(ya_ref, yb_ref))
    y = y * _sigmoid(_dot(y.astype(BF16), wglu_ref[...]) + bglu_ref[...])
    br_b = _dot(y.astype(BF16), wb_ref[...])
    mixed = sa_ref[...] * br_a + sb_ref[...] * br_b
    out_ref[...] = pick(xa_ref, xb_ref) + _dot(mixed.astype(BF16), wout_ref[...])


def _post(xa, xb, oa, ob, ya, yb, proj, g_norm_w, w_glu, b_glu, w_a, w_b, w_out, tm):
    t = xa.shape[0] + xb.shape[0]
    n_first = xa.shape[0] // tm
    assert xa.shape[0] % tm == 0 and xb.shape[0] % tm == 0 and proj.shape[1] == t
    spec_a, spec_b = _two_source_specs(tm, n_first)
    slot = lambda s: pl.BlockSpec((None, tm, D_MODEL), lambda i: (s, i, 0))
    full = lambda shape: pl.BlockSpec(shape, lambda i: (0,) * len(shape))
    sq = (D_MODEL, D_MODEL)
    return pl.pallas_call(
        functools.partial(_post_body, n_first=n_first),
        grid=(t // tm,),
        in_specs=[spec_a, spec_b, spec_a, spec_b, spec_a, spec_b, slot(3), slot(5), slot(6),
                  full((1, HG_DIM)), full(sq), full((1, D_MODEL)), full(sq), full(sq), full(sq)],
        out_specs=pl.BlockSpec((tm, D_MODEL), lambda i: (i, 0)),
        out_shape=jax.ShapeDtypeStruct((t, D_MODEL), F32),
        scratch_shapes=[pltpu.VMEM((tm, D_MODEL), BF16)],
        compiler_params=_params("arbitrary"),
        name="post",
    )(xa, xb, oa, ob, ya, yb, proj, proj, proj, g_norm_w.reshape(1, HG_DIM), w_glu.astype(BF16),
      b_glu.reshape(1, D_MODEL), w_a.astype(BF16), w_b.astype(BF16), w_out.astype(BF16))


def _sort_network(n):
    pairs = []
    p = 1
    while p < n:
        k = p
        while k >= 1:
            for j in range(k % p, n - k, 2 * k):
                for i in range(min(k, n - j - k)):
                    if (i + j) // (2 * p) == (i + j + k) // (2 * p):
                        pairs.append((i + j, i + j + k))
            k //= 2
        p *= 2
    return pairs


def _exchange(vs, i, j):
    vs[i], vs[j] = jnp.maximum(vs[i], vs[j]), jnp.minimum(vs[i], vs[j])


def _merge_top(a, b):
    n = len(a)
    vs = [jnp.maximum(a[i], b[n - 1 - i]) for i in range(n)]
    d = n // 2
    while d >= 1:
        for i in range(n):
            if i & d == 0:
                _exchange(vs, i, i + d)
        d //= 2
    return vs


def _merge_sublanes(vs):
    shift = SUBLANES // 2
    while shift >= 1:
        vs = _merge_top(vs, [pltpu.roll(v, shift, axis=0) for v in vs])
        shift //= 2
    return vs


def _top16_of_keys(s):
    vs = [s[v * SUBLANES:(v + 1) * SUBLANES] for v in range(PEER_NKEYS // SUBLANES)]
    for i, j in _sort_network(len(vs)):
        _exchange(vs, i, j)
    vs = _merge_sublanes(vs)
    return jnp.concatenate([v[0:1] for v in vs], axis=0)


_CAND_COUNTS = tuple(PEER_TOPK // (a + 1) for a in range(PEER_TOPK))


def _candidate_thresholds(v1, v2):
    sub = lax.broadcasted_iota(jnp.int32, (SUBLANES, LANES), 0)
    halves = []
    for half in range(PEER_TOPK // SUBLANES):
        v2h = v2[half * SUBLANES:(half + 1) * SUBLANES]
        halves.append([jnp.where(sub + half * SUBLANES < cnt, v1[a:a + 1] + v2h, -jnp.inf)
                       for a, cnt in enumerate(_CAND_COUNTS)])
    top = _merge_sublanes(_merge_top(halves[0], halves[1]))
    c16 = top[PEER_TOPK - 1][0:1]
    cands = halves[0] + halves[1]
    below = functools.reduce(jnp.maximum, [jnp.where(c < c16, c, -jnp.inf) for c in cands])
    c17 = jnp.max(below, axis=0, keepdims=True)
    return cands, c16, c17


def _peer_body(x_ref, n2_ref, fn_ref, wq_ref, k1_ref, k2_ref, u_ref, vt_ref, ya_ref, yb_ref,
               hnt_scr, s_scr, p_scr, tau_scr, vs_scr, act0, act1, g0, g1, acc_scr,
               *, tm, eb, n_first, nblk):
    i = pl.program_id(0)
    j = pl.program_id(1)
    nlt = tm // LANES
    nrow = eb // PEER_NKEYS
    assert nrow == SUBLANES

    @pl.when(j == 0)
    def _():
        acc_scr[...] = jnp.zeros_like(acc_scr)
        x = x_ref[...]
        ms = jnp.mean(x * x, axis=-1, keepdims=True)
        hn32 = x * lax.rsqrt(ms + EPS) * n2_ref[...]
        hnt_scr[...] = hn32.T.astype(BF16)
        qall = _dot(hn32.astype(BF16), wq_ref[...]).astype(BF16)
        for h in range(PEER_HEADS):
            for half, k_ref in enumerate((k1_ref, k2_ref)):
                c0 = (2 * h + half) * PEER_HALF
                sc = _dot_nt(k_ref[h], qall[:, c0:c0 + PEER_HALF])
                for lt in range(nlt):
                    s_scr[2 * h + half, lt] = sc[:, lt * LANES:(lt + 1) * LANES]

        def top_body(idx, carry):
            hh = idx // nlt
            lt = idx % nlt
            vs_scr[hh, lt] = _top16_of_keys(s_scr[hh, lt])
            return carry

        lax.fori_loop(0, 2 * PEER_HEADS * nlt, top_body, 0)

        def gate_body(idx, carry):
            h = idx // nlt
            lt = idx % nlt
            v1 = vs_scr[2 * h, lt]
            v2 = vs_scr[2 * h + 1, lt]
            cands, c16, c17 = _candidate_thresholds(v1, v2)
            top = v1[0:1] + v2[0:1]
            z = functools.reduce(jnp.add, [jnp.where(c >= c16, jnp.exp(c - top), 0.0) for c in cands])
            z = jnp.sum(z, axis=0, keepdims=True)
            tau_scr[h, lt] = jnp.broadcast_to(0.5 * (c16 + c17), (SUBLANES, LANES))
            s1 = s_scr[2 * h, lt]
            s2 = s_scr[2 * h + 1, lt]
            p_scr[2 * h, lt] = jnp.where(s1 >= v1[PEER_TOPK - 1:], jnp.exp(s1 - v1[0:1]), 0.0) / z
            p_scr[2 * h + 1, lt] = jnp.where(s2 >= v2[PEER_TOPK - 1:], jnp.exp(s2 - v2[0:1]), 0.0)
            return carry

        lax.fori_loop(0, PEER_HEADS * nlt, gate_body, 0)

    e1_rows = pl.ds(pl.multiple_of(jnp.clip(j - 1, 0, nblk - 1) * nrow, nrow), nrow)
    half_keys = PEER_NKEYS // 2

    def stages(act_new, act_cur, g_new, g_cur, first=True, middle=True, final=True):
        def pre_activation(rows):
            act_new[rows, :] = _dot(u_ref[rows, :], hnt_scr[...])

        def accumulate(rows):
            acc_scr[rows, :] += _dot(vt_ref[rows, :], g_cur[...])

        def gated(lt, r0, kh):
            cols = slice(lt * LANES, (lt + 1) * LANES)
            keys = slice(kh * half_keys, (kh + 1) * half_keys)
            ws = [jnp.zeros((half_keys, LANES), F32) for _ in range(2)]
            for h in range(PEER_HEADS):
                s2 = s_scr[2 * h + 1, lt, keys, :]
                p2 = p_scr[2 * h + 1, lt, keys, :]
                s1grp = s_scr[2 * h, lt, e1_rows, :]
                p1grp = p_scr[2 * h, lt, e1_rows, :]
                tau = tau_scr[h, lt][0:1]
                for d in range(2):
                    theta = tau - s1grp[r0 + d:r0 + d + 1]
                    ws[d] = ws[d] + jnp.where(s2 >= theta, p2, 0.0) * p1grp[r0 + d:r0 + d + 1]
            for d in range(2):
                rows = slice((r0 + d) * PEER_NKEYS + kh * half_keys, (r0 + d) * PEER_NKEYS + (kh + 1) * half_keys)
                g_new[rows, cols] = (ws[d] * _gelu_exact(act_cur[rows, cols])).astype(BF16)

        chunk = 4 * LANES
        matmul_jobs = []
        if first:
            matmul_jobs += [functools.partial(pre_activation, slice(m, m + chunk)) for m in range(0, eb, chunk)]
        if final:
            matmul_jobs += [functools.partial(accumulate, slice(m, m + chunk)) for m in range(0, D_MODEL, chunk)]
        gated_jobs = []
        if middle:
            gated_jobs = [functools.partial(gated, lt, r0, kh)
                          for lt in range(nlt) for r0 in range(0, nrow, 2) for kh in range(2)]
        per = -(-len(gated_jobs) // len(matmul_jobs))
        for k, job in enumerate(matmul_jobs):
            job()
            for gjob in gated_jobs[k * per:(k + 1) * per]:
                gjob()

    nstep = nblk + 2
    assert nblk % 2 == 0 and nblk >= 4
    steady = (j >= 2) & (j < nblk)

    @pl.when(j == 0)
    def _():
        stages(act0, act1, g0, g1, middle=False, final=False)

    @pl.when(j == 1)
    def _():
        stages(act1, act0, g1, g0, final=False)

    @pl.when(steady & (j % 2 == 0))
    def _():
        stages(act0, act1, g0, g1)

    @pl.when(steady & (j % 2 == 1))
    def _():
        stages(act1, act0, g1, g0)

    @pl.when(j == nstep - 2)
    def _():
        stages(act0, act1, g0, g1, first=False)

    @pl.when(j == nstep - 1)
    def _():
        stages(act1, act0, g1, g0, first=False, middle=False)

    def finish(y_ref):
        xo = x_ref[...] + acc_scr[...].T
        ms = jnp.mean(xo * xo, axis=-1, keepdims=True)
        y_ref[...] = xo * lax.rsqrt(ms + EPS) * fn_ref[...]

    last = j == pl.num_programs(1) - 1

    @pl.when(last & (i < n_first))
    def _():
        finish(ya_ref)

    @pl.when(last & (i >= n_first))
    def _():
        finish(yb_ref)


def _peer(x, n_first_rows, norm2_w, final_norm_w, wq, k1, k2, u_tab, v_tab, tm, eb):
    t = x.shape[0]
    assert t % tm == 0 and n_first_rows % tm == 0
    n_first = n_first_rows // tm
    nexp = u_tab.shape[0]
    assert nexp == PEER_NKEYS * PEER_NKEYS and nexp % eb == 0
    nblk = nexp // eb
    qw = 2 * PEER_HEADS * PEER_HALF
    nlt = tm // LANES
    full = lambda shape: pl.BlockSpec(shape, lambda i, j: (0,) * len(shape))
    out_a, out_b = _two_source_specs(tm, n_first)
    return pl.pallas_call(
        functools.partial(_peer_body, tm=tm, eb=eb, n_first=n_first, nblk=nblk),
        grid=(t // tm, nblk + 2),
        in_specs=[
            pl.BlockSpec((tm, D_MODEL), lambda i, j: (i, 0)),
            full((1, D_MODEL)), full((1, D_MODEL)),
            full((D_MODEL, qw)),
            full((PEER_HEADS, PEER_NKEYS, PEER_HALF)), full((PEER_HEADS, PEER_NKEYS, PEER_HALF)),
            pl.BlockSpec((eb, D_MODEL), lambda i, j: (jnp.minimum(j, nblk - 1), 0)),
            pl.BlockSpec((None, D_MODEL, eb), lambda i, j: (jnp.clip(j - 2, 0, nblk - 1), 0, 0)),
        ],
        out_specs=[out_a, out_b],
        out_shape=[jax.ShapeDtypeStruct((n_first_rows, D_MODEL), F32),
                   jax.ShapeDtypeStruct((t - n_first_rows, D_MODEL), F32)],
        scratch_shapes=[
            pltpu.VMEM((D_MODEL, tm), BF16),
            pltpu.VMEM((2 * PEER_HEADS, nlt, PEER_NKEYS, LANES), F32),
            pltpu.VMEM((2 * PEER_HEADS, nlt, PEER_NKEYS, LANES), F32),
            pltpu.VMEM((PEER_HEADS, nlt, SUBLANES, LANES), F32),
            pltpu.VMEM((2 * PEER_HEADS, nlt, PEER_TOPK, LANES), F32),
            pltpu.VMEM((eb, tm), F32), pltpu.VMEM((eb, tm), F32),
            pltpu.VMEM((eb, tm), BF16), pltpu.VMEM((eb, tm), BF16),
            pltpu.VMEM((D_MODEL, tm), F32),
        ],
        compiler_params=_params("arbitrary", "arbitrary"),
        name="peer",
    )(x, norm2_w.reshape(1, D_MODEL), final_norm_w.reshape(1, D_MODEL), wq.astype(BF16),
      k1.astype(BF16), k2.astype(BF16), u_tab.astype(BF16),
      jnp.transpose(v_tab.astype(BF16).reshape(nblk, eb, D_MODEL), (0, 2, 1)))


def kernel(x_prompt, x_sample, state_hgrn, state_ssm_re, state_ssm_im, meta_tokens, lower_bounds,
           norm1_w, w_in, g_norm_w, ssm_a_re, ssm_a_im, ssm_log_step, ssm_b_re, ssm_b_im,
           ssm_c_re, ssm_c_im, ssm_d, w_glu, b_glu, w_branch_a, w_branch_b, w_out, norm2_w,
           peer_wq, peer_k1, peer_k2, peer_u, peer_v, final_norm_w):
    depth = w_in.shape[0]
    assert depth == 1, "single-layer step only"
    nb, seq, _ = x_prompt.shape
    ns, dseq, _ = x_sample.shape
    tp = nb * seq
    tsm = ns * dseq
    tm = 512
    tm_in = 256
    hg_c = 128
    small_c = 16
    s5_tok = 8
    s5_seq = 4
    assert tp % tm == 0 and tsm % tm == 0 and tm % tm_in == 0 and seq % hg_c == 0 and seq % s5_tok == 0
    assert nb % s5_seq == 0 and dseq <= small_c and dseq % 2 == 0 and N_META % s5_tok == 0

    w_in_b = w_in[0].astype(BF16)
    xp = x_prompt.reshape(tp, D_MODEL).astype(F32)
    xs = x_sample.reshape(tsm, D_MODEL).astype(F32)
    proj, lg = _inproj(xp, xs, norm1_w[0], lower_bounds, w_in_b, tm_in)
    proj_m, lg_m = _inproj(meta_tokens.astype(F32), None, norm1_w[0], lower_bounds, w_in_b, N_META)

    zero_hg = jnp.zeros((1, HG_HEADS, HG_DIM, HG_DIM), F32)
    _, hg_meta = _hgrn(proj_m, lg_m, zero_hg, 1, 1, N_META, 0)
    o_p, hg_p = _hgrn(proj, lg, hg_meta, nb, seq // hg_c, hg_c, 0)
    o_s, hg_s = _hgrn(proj, lg, state_hgrn[0].astype(F32), ns, 1, small_c, tp, ntok=dseq)
    o_s = o_s.reshape(tsm, D_MODEL)

    ssm = (ssm_a_re[0], ssm_a_im[0], ssm_log_step[0], ssm_b_re[0], ssm_b_im[0], ssm_c_re[0],
           ssm_c_im[0], ssm_d[0])
    compact = _s5_pieces(*ssm, max(s5_tok, dseq))
    ops_p = _s5_operators(compact, s5_tok)
    ops_s = _s5_operators(compact, dseq)
    nst = SSM_G * SSM_P
    u_meta = jnp.pad(proj_m[4], ((0, 7 * N_META), (0, 0)))
    zero_ss = jnp.zeros((1, SUBLANES, nst), F32)
    _, mre, mim = _s5(u_meta, 0, ops_p, zero_ss, zero_ss, 0, 1, SUBLANES, N_META // s5_tok, s5_tok, 3)
    nblk = nb // s5_seq
    x0re = jnp.broadcast_to(mre[:, 0:1], (nblk, SUBLANES, nst))
    x0im = jnp.broadcast_to(mim[:, 0:1], (nblk, SUBLANES, nst))
    y_p, pre, pim = _s5(proj, 4, ops_p, x0re, x0im, 0, nblk, s5_seq, seq // s5_tok, s5_tok, 1)
    y_s, sre, sim = _s5(proj, 4, ops_s, state_ssm_re[0].astype(F32).reshape(1, ns, nst),
                        state_ssm_im[0].astype(F32).reshape(1, ns, nst), tp, 1, ns, 1, dseq, 3)

    x1 = _post(xp, xs, o_p, o_s, y_p, y_s, proj, g_norm_w[0], w_glu[0], b_glu[0], w_branch_a[0],
               w_branch_b[0], w_out[0], tm)
    y_pr, y_sm = _peer(x1, tp, norm2_w[0], final_norm_w, peer_wq[0], peer_k1[0], peer_k2[0],
                       peer_u[0], peer_v[0], tm, SUBLANES * PEER_NKEYS)

    sd = state_hgrn.dtype
    st = lambda a: a[:, :s5_seq].reshape(nb, SSM_G, SSM_P)[None]
    return (y_pr.reshape(nb, seq, D_MODEL).astype(x_prompt.dtype),
            y_sm.reshape(ns, dseq, D_MODEL).astype(x_sample.dtype),
            hg_p[None].astype(sd),
            st(pre).astype(state_ssm_re.dtype),
            st(pim).astype(state_ssm_im.dtype),
            hg_s[None].astype(sd),
            sre.reshape(1, ns, SSM_G, SSM_P).astype(state_ssm_re.dtype),
            sim.reshape(1, ns, SSM_G, SSM_P).astype(state_ssm_im.dtype))
```

```python
import functools
import math

import numpy as np
import jax
import jax.numpy as jnp
from jax import lax
from jax.experimental import pallas as pl
from jax.experimental.pallas import tpu as pltpu

F32 = jnp.float32
BF16 = jnp.bfloat16

D_MODEL = 1024
N_META = 16
HG_HEADS = 8
HG_DIM = 128
SSM_G = 64
SSM_GROUP = 16
SSM_P = 64
SSM_MIN_RE = -1e-4
S5_GB = 8
PEER_HEADS = 8
PEER_NKEYS = 128
PEER_TOPK = 16
PEER_HALF = 128
EPS = 1e-6
LANES = 128
SUBLANES = 8
VMEM_LIMIT = 56 * 1024 * 1024


def _dot(a, b):
    return jnp.dot(a, b, preferred_element_type=F32)


def _dot_nt(a, b):
    return lax.dot_general(a, b, (((1,), (1,)), ((), ())), preferred_element_type=F32)


def _dot_tn(a, b):
    return lax.dot_general(a, b, (((0,), (0,)), ((), ())), preferred_element_type=F32)


def _split_bf16(x):
    hi = x.astype(BF16)
    lo = (x - hi.astype(F32)).astype(BF16)
    return hi, lo


def _sigmoid(x):
    return 1.0 / (1.0 + jnp.exp(-x))


def _gelu_exact(x):
    return 0.5 * x * (1.0 + lax.erf(x * (1.0 / math.sqrt(2.0))))


def _params(*sem):
    return pltpu.CompilerParams(dimension_semantics=sem, vmem_limit_bytes=VMEM_LIMIT)


def _two_source_specs(tm, n_first, ncols=D_MODEL):
    first = pl.BlockSpec((tm, ncols), lambda i, *_: (jnp.minimum(i, n_first - 1), 0))
    second = pl.BlockSpec((tm, ncols), lambda i, *_: (jnp.maximum(i - n_first, 0), 0))
    return first, second


N_SEG = 7


def _inproj_body(xa_ref, xb_ref, nw_ref, lbs_ref, w_ref, o_ref, lg_ref, *, n_first):
    x = jnp.where(pl.program_id(0) < n_first, xa_ref[...], xb_ref[...])
    ms = jnp.mean(x * x, axis=-1, keepdims=True)
    h = (x * lax.rsqrt(ms + EPS) * nw_ref[...]).astype(BF16)
    lbs = lbs_ref[...]
    e = jnp.exp(lbs - jnp.max(lbs, axis=0, keepdims=True))
    lb = e[0:1] / jnp.sum(e, axis=0, keepdims=True)
    for seg in range(N_SEG):
        p = _dot(h, w_ref[:, seg * D_MODEL:(seg + 1) * D_MODEL])
        if seg in (0, 3):
            o_ref[seg] = p * _sigmoid(p)
        elif seg == 1:
            fg = lb + (1.0 - lb) * _sigmoid(p)
            o_ref[seg] = 1.0 - fg
            lg_ref[...] = jnp.log(fg)
        elif seg in (2, 4):
            o_ref[seg] = p
        else:
            o_ref[seg] = _sigmoid(p)


def _inproj(xa, xb, norm_w, lower_bounds, w_in_bf16, tm):
    if xb is None:
        xb = xa
        t = xa.shape[0]
        n_first = t // tm
    else:
        t = xa.shape[0] + xb.shape[0]
        n_first = xa.shape[0] // tm
        assert xa.shape[0] % tm == 0 and xb.shape[0] % tm == 0
    spec_a, spec_b = _two_source_specs(tm, n_first)
    return pl.pallas_call(
        functools.partial(_inproj_body, n_first=n_first),
        grid=(t // tm,),
        in_specs=[
            spec_a, spec_b,
            pl.BlockSpec((1, D_MODEL), lambda i: (0, 0)),
            pl.BlockSpec(lower_bounds.shape, lambda i: (0, 0)),
            pl.BlockSpec((D_MODEL, N_SEG * D_MODEL), lambda i: (0, 0)),
        ],
        out_specs=[
            pl.BlockSpec((N_SEG, tm, D_MODEL), lambda i: (0, i, 0)),
            pl.BlockSpec((tm, D_MODEL), lambda i: (i, 0)),
        ],
        out_shape=[
            jax.ShapeDtypeStruct((N_SEG, t, D_MODEL), F32),
            jax.ShapeDtypeStruct((t, D_MODEL), F32),
        ],
        compiler_params=_params("arbitrary"),
        name="inproj",
    )(xa, xb, norm_w.reshape(1, D_MODEL), lower_bounds, w_in_bf16)


def _hgrn_consts(c):
    nlev = int(round(math.log2(c)))
    assert 1 << nlev == c and c % SUBLANES == 0
    tri = np.tril(np.ones((c, c), np.float32))
    lev = np.full((c, c), -1, np.int32)
    for t in range(c):
        for s in range(t + 1):
            lev[t, s] = (t ^ s).bit_length()
    return tri, lev, nlev


def _decay_factors(lg, tri, c, nlev):
    hi, lo = _split_bf16(lg)
    b = _dot(tri, hi) + _dot(tri, lo)
    row = lambda r: jnp.broadcast_to(b[r:r + 1], (SUBLANES, HG_DIM))
    sub = lax.broadcasted_iota(jnp.int32, (SUBLANES, HG_DIM), 0)
    tidx = lax.broadcasted_iota(jnp.int32, (c, HG_DIM), 0)
    out = [jnp.exp(b), jnp.exp(b[c - 1:c] - b)]
    for l in range(1, nlev + 1):
        blk = 1 << l
        half = blk >> 1
        upper = ((tidx >> (l - 1)) & 1) == 1
        if l == 1:
            e = jnp.where(upper, lg, 0.0)
        else:
            groups = []
            for t0 in range(0, c, SUBLANES):
                if blk >= SUBLANES:
                    groups.append(row((t0 // blk) * blk + half - 1))
                else:
                    groups.append(jnp.where(sub < blk, row(t0 + half - 1), row(t0 + blk + half - 1)))
            bref = jnp.concatenate(groups, axis=0)
            e = jnp.where(upper, b - bref, bref - b)
        out.append(jnp.exp(e))
    return out, hi, lo


def _hgrn_body(q_ref, k_ref, v_ref, lg_ref, s0_ref, d_ref, lev_ref, o_ref, sout_ref, s_scr,
               *pad_scr, c, nlev, ntok, nsub):
    n = pl.program_id(0)
    r = pl.program_id(1)

    @pl.when(r == 0)
    def _():
        for sub in range(nsub):
            s_scr[sub] = s0_ref[sub if s0_ref.shape[0] == nsub else 0]

    if ntok < c:
        (pad,) = pad_scr

        @pl.when((n == 0) & (r == 0))
        def _():
            pad[...] = jnp.zeros_like(pad)

    tri = d_ref[...]
    lev = lev_ref[...]
    ones = jnp.ones((c, HG_DIM), BF16)
    tidx = lax.broadcasted_iota(jnp.int32, (c, HG_DIM), 0)
    for sub in range(nsub):
        rows = slice(sub * ntok, (sub + 1) * ntok)
        if ntok < c:
            for a, ref in enumerate((q_ref, k_ref, v_ref, lg_ref)):
                pad[sub, a, 0:ntok, :] = ref[rows, :]
            qr, kr, vr, lr = (pad.at[sub, a] for a in range(4))
        else:
            qr, kr, vr, lr = q_ref, k_ref, v_ref, lg_ref
        heads = [slice(h * HG_DIM, (h + 1) * HG_DIM) for h in range(HG_HEADS)]
        facs = [_decay_factors(lr[:, hs], tri, c, nlev) for hs in heads]
        xs = [f[0] for f in facs]
        lgs = [(f[1], f[2]) for f in facs]
        atts = []
        for hs, x in zip(heads, xs):
            q = qr[:, hs]
            k = kr[:, hs]
            att = jnp.where(lev == 0, _dot_nt(q.astype(BF16), k.astype(BF16)), 0.0)
            for l in range(1, nlev + 1):
                upper = ((tidx >> (l - 1)) & 1) == 1
                m = (jnp.where(upper, q, k) * x[1 + l]).astype(BF16)
                att = jnp.where(lev == l, _dot_nt(m, m), att)
            atts.append(att.astype(BF16))
        for h, (hs, x, att) in enumerate(zip(heads, xs, atts)):
            qd = (qr[:, hs] * x[0]).astype(BF16)
            o = _dot(qd, s_scr[sub, h].astype(BF16)) + _dot(att, vr[:, hs].astype(BF16))
            o_ref[rows, hs] = o[0:ntok]
        for h, (hs, x, (hi, lo)) in enumerate(zip(heads, xs, lgs)):
            kd = (kr[:, hs] * x[1]).astype(BF16)
            gl = _dot_tn(hi, ones) + _dot_tn(lo, ones)
            s_scr[sub, h] = jnp.exp(gl) * s_scr[sub, h] + _dot_tn(kd, vr[:, hs].astype(BF16))

    @pl.when(r == pl.num_programs(1) - 1)
    def _():
        sout_ref[...] = s_scr[...]


def _hgrn(proj, lg, s0, nseq, nchunks, c, row0, ntok=None):
    ntok = c if ntok is None else ntok
    dnp, levnp, nlev = _hgrn_consts(c)
    assert ntok == c or nchunks == 1
    nsub = 1 if ntok == c else (2 * SUBLANES) // ntok
    step_rows = nsub * ntok
    assert row0 % step_rows == 0 and nseq % nsub == 0 and step_rows % SUBLANES == 0
    blk0 = row0 // step_rows
    tok_specs = [pl.BlockSpec((None, step_rows, D_MODEL),
                              functools.partial(lambda n, r, s: (s, blk0 + n * nchunks + r, 0), s=s))
                 for s in range(3)]
    lg_spec = pl.BlockSpec((step_rows, D_MODEL), lambda n, r: (blk0 + n * nchunks + r, 0))
    o_spec = pl.BlockSpec((step_rows, D_MODEL), lambda n, r: (n * nchunks + r, 0))
    o_shape = jax.ShapeDtypeStruct((nseq * nchunks * ntok, D_MODEL), F32)
    scratch = [] if ntok == c else [pltpu.VMEM((nsub, 4, c, D_MODEL), F32)]
    st_blk = (nsub, HG_HEADS, HG_DIM, HG_DIM)
    if s0.shape[0] == 1:
        s0_spec = pl.BlockSpec((1,) + st_blk[1:], lambda n, r: (0, 0, 0, 0))
    else:
        s0_spec = pl.BlockSpec(st_blk, lambda n, r: (n, 0, 0, 0))
    return pl.pallas_call(
        functools.partial(_hgrn_body, c=c, nlev=nlev, ntok=ntok, nsub=nsub),
        grid=(nseq // nsub, nchunks),
        in_specs=tok_specs + [
            lg_spec, s0_spec,
            pl.BlockSpec(dnp.shape, lambda n, r: (0, 0)),
            pl.BlockSpec(levnp.shape, lambda n, r: (0, 0)),
        ],
        out_specs=[o_spec, pl.BlockSpec(st_blk, lambda n, r: (n, 0, 0, 0))],
        out_shape=[o_shape, jax.ShapeDtypeStruct((nseq, HG_HEADS, HG_DIM, HG_DIM), F32)],
        scratch_shapes=[pltpu.VMEM(st_blk, F32)] + scratch,
        compiler_params=_params("arbitrary", "arbitrary"),
        name=f"hgrn_c{c}_t{ntok}",
    )(proj, proj, proj, lg, s0, jnp.asarray(dnp, BF16), jnp.asarray(levnp))


def _s5_pieces_body(are_ref, aim_ref, ls_ref, btre_ref, btim_ref, ccre_ref, ccim_ref, d_ref,
                     kl_ref, wre_ref, wim_ref, wore_ref, woim_ref, pwre_ref, pwim_ref, *, nmax):
    hp = lax.Precision.HIGHEST
    rows = S5_GB * SSM_GROUP
    lanes = S5_GB * SSM_P
    lam_re = jnp.minimum(are_ref[...], SSM_MIN_RE)
    lam_im = aim_ref[...]
    delta = jnp.exp(ls_ref[...])
    pw_re, pw_im = [], []
    for n in range(nmax + 1):
        mag = jnp.exp(lam_re * delta * float(n))
        ang = lam_im * delta * float(n)
        pw_re.append(mag * jnp.cos(ang))
        pw_im.append(mag * jnp.sin(ang))
        pwre_ref[n] = pw_re[n]
        pwim_ref[n] = pw_im[n]
    x = pw_re[1] - 1.0
    y = pw_im[1]
    den = lam_re * lam_re + lam_im * lam_im
    cr = (x * lam_re + y * lam_im) / den
    ci = (y * lam_re - x * lam_im) / den
    bb_re = cr * btre_ref[...] - ci * btim_ref[...]
    bb_im = cr * btim_ref[...] + ci * btre_ref[...]
    cc_re = ccre_ref[...]
    cc_im = ccim_ref[...]
    row_g = lax.broadcasted_iota(jnp.int32, (rows, rows), 0) // SSM_GROUP
    same_c = row_g == lax.broadcasted_iota(jnp.int32, (rows, rows), 1) // SSM_GROUP
    eye = (lax.broadcasted_iota(jnp.int32, (rows, rows), 0)
           == lax.broadcasted_iota(jnp.int32, (rows, rows), 1)).astype(F32)
    same_p = (lax.broadcasted_iota(jnp.int32, (rows, lanes), 0) // SSM_GROUP
              == lax.broadcasted_iota(jnp.int32, (rows, lanes), 1) // SSM_P)
    rep = (lax.broadcasted_iota(jnp.int32, (SSM_P, lanes), 0)
           == lax.broadcasted_iota(jnp.int32, (SSM_P, lanes), 1) % SSM_P).astype(F32)
    def exact3(t):
        t1 = t.astype(BF16)
        r1 = t - t1.astype(F32)
        t2 = r1.astype(BF16)
        return t1, t2, (r1 - t2.astype(F32)).astype(BF16)

    rep_b = rep.astype(BF16)
    eye_b = eye.astype(BF16)
    spread = lambda t: jnp.where(same_p, sum(_dot(part, rep_b) for part in exact3(t)), 0.0)
    nt = lambda u, v: lax.dot_general(u, v, (((1,), (1,)), ((), ())), precision=hp)
    tn = lambda u, v: sum(_dot_tn(part, v) for part in exact3(u))
    for n in range(nmax):
        w_re = pw_re[n] * bb_re - pw_im[n] * bb_im
        w_im = pw_re[n] * bb_im + pw_im[n] * bb_re
        kl = jnp.where(same_c, nt(w_re, cc_re) - nt(w_im, cc_im), 0.0)
        if n == 0:
            kl = kl + d_ref[...] * eye
        kl_ref[n] = kl
        wre_ref[nmax - 1 - n] = spread(w_re)
        wim_ref[nmax - 1 - n] = spread(w_im)
        o_re = cc_re * pw_re[n + 1] - cc_im * pw_im[n + 1]
        o_im = -(cc_re * pw_im[n + 1] + cc_im * pw_re[n + 1])
        wore_ref[n] = tn(spread(o_re), eye_b)
        woim_ref[n] = tn(spread(o_im), eye_b)


def _s5_pieces(a_re, a_im, log_step, b_re, b_im, c_re, c_im, d_skip, nmax):
    nb = SSM_G // S5_GB
    rows = S5_GB * SSM_GROUP
    lanes = S5_GB * SSM_P
    per_row = lambda a: jnp.repeat(a.astype(F32), SSM_GROUP, axis=0)
    flat = lambda a: a.astype(F32).reshape(SSM_G * SSM_GROUP, SSM_P)
    tile = pl.BlockSpec((rows, SSM_P), lambda g: (g, 0))
    out4 = lambda r, c: pl.BlockSpec((nmax, None, r, c), lambda g: (0, g, 0, 0))
    shape4 = lambda r, c: jax.ShapeDtypeStruct((nmax, nb, r, c), F32)
    pw_spec = pl.BlockSpec((nmax + 1, rows, SSM_P), lambda g: (0, g, 0))
    pw_shape = jax.ShapeDtypeStruct((nmax + 1, SSM_G * SSM_GROUP, SSM_P), F32)
    kl, w_re, w_im, wo_re, wo_im, pw_re, pw_im = pl.pallas_call(
        functools.partial(_s5_pieces_body, nmax=nmax),
        grid=(nb,),
        in_specs=[tile] * 7 + [pl.BlockSpec((rows, rows), lambda g: (g, 0))],
        out_specs=[out4(rows, rows), out4(rows, lanes), out4(rows, lanes), out4(lanes, rows),
                   out4(lanes, rows), pw_spec, pw_spec],
        out_shape=[shape4(rows, rows), shape4(rows, lanes), shape4(rows, lanes), shape4(lanes, rows),
                   shape4(lanes, rows), pw_shape, pw_shape],
        compiler_params=_params("arbitrary"),
        name="s5_pieces",
    )(per_row(a_re), per_row(a_im),
      per_row(jnp.broadcast_to(log_step.astype(F32)[:, None], (SSM_G, SSM_P))),
      flat(jnp.swapaxes(b_re, 1, 2)), flat(jnp.swapaxes(b_im, 1, 2)), flat(c_re), flat(c_im),
      jnp.broadcast_to(d_skip.astype(F32).reshape(SSM_G * SSM_GROUP, 1), (SSM_G * SSM_GROUP, rows)))
    per_group = lambda p: p[:, ::SSM_GROUP, :].reshape(nmax + 1, nb, lanes)
    return dict(kl=kl, w_re=w_re, w_im=w_im, wo_re=wo_re, wo_im=wo_im,
                pw_re=per_group(pw_re), pw_im=per_group(pw_im))


def _s5_operators(cp, ntok):
    nb = SSM_G // S5_GB
    kl = cp['kl']
    zero = jnp.zeros_like(kl[0])
    tz = jnp.stack([jnp.concatenate([kl[t - s] if t >= s else zero for t in range(ntok)], axis=-1)
                    for s in range(ntok)], axis=1)
    pair = lambda a: a.reshape(nb, ntok // 2, 2 * LANES, a.shape[-1])
    state_in = lambda w: pair(jnp.swapaxes(w[w.shape[0] - ntok:], 0, 1))
    state_out = lambda w: jnp.concatenate(list(w[:ntok]), axis=-1)
    apow = jnp.stack([cp['pw_re'][ntok], cp['pw_im'][ntok]], axis=1)
    return dict(tz=pair(tz).astype(BF16), wst_re=state_in(cp['w_re']), wst_im=state_in(cp['w_im']),
                wo_re=state_out(cp['wo_re']).astype(BF16), wo_im=state_out(cp['wo_im']).astype(BF16),
                apow=apow)


def _s5_body(u_ref, tz_ref, wre_ref, wim_ref, wore_ref, woim_ref, a_ref, x0re_ref, x0im_ref,
             y_ref, xre_ref, xim_ref, zre_scr, zim_scr, ub_scr,
             *, ntok, nchunks, nseq, npass):
    rows = nseq * nchunks
    nlb = (S5_GB * SSM_P) // LANES
    lane_blk = lambda lb: slice(lb * LANES, (lb + 1) * LANES)
    for sp in range(ntok // 2):
        us = jnp.concatenate([u_ref[pl.ds(2 * sp + d, rows, stride=ntok), :] for d in range(2)],
                             axis=1)
        uh, ul = _split_bf16(us)
        ub_scr[sp] = uh
        for w_ref, z_scr in ((wre_ref, zre_scr), (wim_ref, zim_scr)):
            wh, wl = _split_bf16(w_ref[sp])
            z = _dot(uh, wh)
            if npass >= 2:
                z = z + _dot(ul, wh)
            if npass >= 3:
                z = z + _dot(uh, wl)
            for lb in range(nlb):
                if sp == 0:
                    z_scr[lb] = z[:, lane_blk(lb)]
                else:
                    z_scr[lb] += z[:, lane_blk(lb)]
    a_re = a_ref[0:1, :]
    a_im = a_ref[1:2, :]

    def step(r, carry):
        x_re, x_im = carry
        at = pl.ds(r, nseq, stride=nchunks) if nchunks > 1 else pl.ds(0, nseq)
        z_re = jnp.concatenate([zre_scr[lb, at, :] for lb in range(nlb)], axis=1)
        z_im = jnp.concatenate([zim_scr[lb, at, :] for lb in range(nlb)], axis=1)
        for lb in range(nlb):
            zre_scr[lb, at, :] = x_re[:, lane_blk(lb)]
            zim_scr[lb, at, :] = x_im[:, lane_blk(lb)]
        return (a_re * x_re - a_im * x_im + z_re, a_re * x_im + a_im * x_re + z_im)

    x_re, x_im = lax.fori_loop(0, nchunks, step, (x0re_ref[0:nseq, :], x0im_ref[0:nseq, :]),
                               unroll=2 if nchunks % 2 == 0 else 1)
    xre_ref[...] = jnp.zeros_like(xre_ref)
    xim_ref[...] = jnp.zeros_like(xim_ref)
    xre_ref[0:nseq, :] = x_re
    xim_ref[0:nseq, :] = x_im
    for t0 in range(0, ntok, 2):
        cols = slice(t0 * LANES, (t0 + 2) * LANES)
        acc = jnp.zeros((rows, 2 * LANES), F32)
        for lb in range(0, nlb, 2):
            krows = slice(lb * LANES, (lb + 2) * LANES)
            for z_scr, wo_ref in ((zre_scr, wore_ref), (zim_scr, woim_ref)):
                xin = jnp.concatenate([z_scr[lb], z_scr[lb + 1]], axis=1).astype(BF16)
                acc = acc + _dot(xin, wo_ref[krows, cols])
        for sp in range(t0 // 2 + 1):
            acc = acc + _dot(ub_scr[sp], tz_ref[sp, :, cols])
        y_ref[pl.ds(t0, rows, stride=ntok), :] = acc[:, :LANES]
        y_ref[pl.ds(t0 + 1, rows, stride=ntok), :] = acc[:, LANES:]


def _s5(u, slot, ops, x0_re, x0_im, row0, nblk, nseq, nchunks, ntok, npass):
    nb = SSM_G // S5_GB
    rows = nseq * nchunks
    trows = rows * ntok
    assert row0 % trows == 0 and ntok % 2 == 0
    blk0 = row0 // trows
    nseq_pad = x0_re.shape[1]
    lanes = S5_GB * SSM_P
    if u.ndim == 3:
        u_spec = pl.BlockSpec((None, trows, LANES), lambda g, sb: (slot, blk0 + sb, g))
    else:
        u_spec = pl.BlockSpec((trows, LANES), lambda g, sb: (blk0 + sb, g))
    per_g = lambda shape: pl.BlockSpec((None,) + shape, lambda g, sb: (g,) + (0,) * len(shape))
    st_spec = pl.BlockSpec((None, nseq_pad, lanes), lambda g, sb: (sb, 0, g))
    st_shape = jax.ShapeDtypeStruct((nblk, nseq_pad, SSM_G * SSM_P), F32)
    return pl.pallas_call(
        functools.partial(_s5_body, ntok=ntok, nchunks=nchunks, nseq=nseq, npass=npass),
        grid=(nb, nblk),
        in_specs=[
            u_spec,
            per_g((ntok // 2, 2 * LANES, ntok * LANES)),
            per_g((ntok // 2, 2 * LANES, lanes)), per_g((ntok // 2, 2 * LANES, lanes)),
            per_g((lanes, ntok * LANES)), per_g((lanes, ntok * LANES)),
            per_g((2, lanes)),
            st_spec, st_spec,
        ],
        out_specs=[pl.BlockSpec((trows, LANES), lambda g, sb: (sb, g)), st_spec, st_spec],
        out_shape=[jax.ShapeDtypeStruct((nblk * trows, D_MODEL), F32), st_shape, st_shape],
        scratch_shapes=[pltpu.VMEM((lanes // LANES, rows, LANES), F32),
                        pltpu.VMEM((lanes // LANES, rows, LANES), F32),
                        pltpu.VMEM((ntok // 2, rows, 2 * LANES), BF16)],
        compiler_params=_params("arbitrary", "arbitrary"),
        name=f"s5_r{rows}_t{ntok}",
    )(u, ops['tz'], ops['wst_re'], ops['wst_im'], ops['wo_re'], ops['wo_im'], ops['apow'],
      x0_re, x0_im)


def _post_body(xa_ref, xb_ref, oa_ref, ob_ref, ya_ref, yb_ref, sg_ref, sa_ref, sb_ref, gnw_ref,
               wglu_ref, bglu_ref, wa_ref, wb_ref, wout_ref, out_ref, a_scr, *, n_first):
    first = pl.program_id(0) < n_first
    pick = lambda a_ref, b_ref: jnp.where(first, a_ref[...], b_ref[...])
    gnw = gnw_ref[...]
    o = pick(oa_ref, ob_ref)
    for h in range(HG_HEADS):
        hs = slice(h * HG_DIM, (h + 1) * HG_DIM)
        oh = o[:, hs]
        ms = jnp.mean(oh * oh, axis=-1, keepdims=True)
        a_scr[:, hs] = (oh * lax.rsqrt(ms + EPS) * gnw * sg_ref[:, hs]).astype(BF16)
    br_a = _dot(a_scr[...], wa_ref[...])
    y = _gelu_exact(pick(ya_ref, yb_ref))
    y = y * _sigmoid(_dot(y.astype(BF16), wglu_ref[...]) + bglu_ref[...])
    br_b = _dot(y.astype(BF16), wb_ref[...])
    mixed = sa_ref[...] * br_a + sb_ref[...] * br_b
    out_ref[...] = pick(xa_ref, xb_ref) + _dot(mixed.astype(BF16), wout_ref[...])


def _post(xa, xb, oa, ob, ya, yb, proj, g_norm_w, w_glu, b_glu, w_a, w_b, w_out, tm):
    t = xa.shape[0] + xb.shape[0]
    n_first = xa.shape[0] // tm
    assert xa.shape[0] % tm == 0 and xb.shape[0] % tm == 0 and proj.shape[1] == t
    spec_a, spec_b = _two_source_specs(tm, n_first)
    slot = lambda s: pl.BlockSpec((None, tm, D_MODEL), lambda i: (s, i, 0))
    full = lambda shape: pl.BlockSpec(shape, lambda i: (0,) * len(shape))
    sq = (D_MODEL, D_MODEL)
    return pl.pallas_call(
        functools.partial(_post_body, n_first=n_first),
        grid=(t // tm,),
        in_specs=[spec_a, spec_b, spec_a, spec_b, spec_a, spec_b, slot(3), slot(5), slot(6),
                  full((1, HG_DIM)), full(sq), full((1, D_MODEL)), full(sq), full(sq), full(sq)],
        out_specs=pl.BlockSpec((tm, D_MODEL), lambda i: (i, 0)),
        out_shape=jax.ShapeDtypeStruct((t, D_MODEL), F32),
        scratch_shapes=[pltpu.VMEM((tm, D_MODEL), BF16)],
        compiler_params=_params("arbitrary"),
        name="post",
    )(xa, xb, oa, ob, ya, yb, proj, proj, proj, g_norm_w.reshape(1, HG_DIM), w_glu.astype(BF16),
      b_glu.reshape(1, D_MODEL), w_a.astype(BF16), w_b.astype(BF16), w_out.astype(BF16))


def _sort_network(n):
    pairs = []
    p = 1
    while p < n:
        k = p
        while k >= 1:
            for j in range(k % p, n - k, 2 * k):
                for i in range(min(k, n - j - k)):
                    if (i + j) // (2 * p) == (i + j + k) // (2 * p):
                        pairs.append((i + j, i + j + k))
            k //= 2
        p *= 2
    return pairs


def _exchange(vs, i, j):
    vs[i], vs[j] = jnp.maximum(vs[i], vs[j]), jnp.minimum(vs[i], vs[j])


def _merge_top(a, b):
    n = len(a)
    vs = [jnp.maximum(a[i], b[n - 1 - i]) for i in range(n)]
    d = n // 2
    while d >= 1:
        for i in range(n):
            if i & d == 0:
                _exchange(vs, i, i + d)
        d //= 2
    return vs


def _merge_sublanes(vs):
    shift = SUBLANES // 2
    while shift >= 1:
        vs = _merge_top(vs, [pltpu.roll(v, shift, axis=0) for v in vs])
        shift //= 2
    return vs


def _top16_of_keys(s):
    vs = [s[v * SUBLANES:(v + 1) * SUBLANES] for v in range(PEER_NKEYS // SUBLANES)]
    for i, j in _sort_network(len(vs)):
        _exchange(vs, i, j)
    vs = _merge_sublanes(vs)
    return jnp.concatenate([v[0:1] for v in vs], axis=0)


_CAND_COUNTS = tuple(PEER_TOPK // (a + 1) for a in range(PEER_TOPK))


def _candidate_thresholds(v1, v2):
    sub = lax.broadcasted_iota(jnp.int32, (SUBLANES, LANES), 0)
    halves = []
    for half in range(PEER_TOPK // SUBLANES):
        v2h = v2[half * SUBLANES:(half + 1) * SUBLANES]
        halves.append([jnp.where(sub + half * SUBLANES < cnt, v1[a:a + 1] + v2h, -jnp.inf)
                       for a, cnt in enumerate(_CAND_COUNTS)])
    top = _merge_sublanes(_merge_top(halves[0], halves[1]))
    c16 = top[PEER_TOPK - 1][0:1]
    cands = halves[0] + halves[1]
    below = functools.reduce(jnp.maximum, [jnp.where(c < c16, c, -jnp.inf) for c in cands])
    c17 = jnp.max(below, axis=0, keepdims=True)
    return cands, c16, c17


def _peer_body(x_ref, n2_ref, fn_ref, wq_ref, k1_ref, k2_ref, u_ref, vt_ref, ya_ref, yb_ref,
               hnt_scr, s_scr, p_scr, tau_scr, vs_scr, act0, act1, g0, g1, acc_scr,
               *, tm, eb, n_first, nblk):
    i = pl.program_id(0)
    j = pl.program_id(1)
    nlt = tm // LANES
    nrow = eb // PEER_NKEYS
    assert nrow == SUBLANES

    @pl.when(j == 0)
    def _():
        acc_scr[...] = jnp.zeros_like(acc_scr)
        x = x_ref[...]
        ms = jnp.mean(x * x, axis=-1, keepdims=True)
        hn32 = x * lax.rsqrt(ms + EPS) * n2_ref[...]
        hnt_scr[...] = hn32.T.astype(BF16)
        qall = _dot(hn32.astype(BF16), wq_ref[...]).astype(BF16)
        for h in range(PEER_HEADS):
            for half, k_ref in enumerate((k1_ref, k2_ref)):
                c0 = (2 * h + half) * PEER_HALF
                sc = _dot_nt(k_ref[h], qall[:, c0:c0 + PEER_HALF])
                for lt in range(nlt):
                    s_scr[2 * h + half, lt] = sc[:, lt * LANES:(lt + 1) * LANES]

        def top_body(idx, carry):
            hh = idx // nlt
            lt = idx % nlt
            vs_scr[hh, lt] = _top16_of_keys(s_scr[hh, lt])
            return carry

        lax.fori_loop(0, 2 * PEER_HEADS * nlt, top_body, 0)

        def gate_body(idx, carry):
            h = idx // nlt
            lt = idx % nlt
            v1 = vs_scr[2 * h, lt]
            v2 = vs_scr[2 * h + 1, lt]
            cands, c16, c17 = _candidate_thresholds(v1, v2)
            top = v1[0:1] + v2[0:1]
            z = functools.reduce(jnp.add, [jnp.where(c >= c16, jnp.exp(c - top), 0.0) for c in cands])
            z = jnp.sum(z, axis=0, keepdims=True)
            tau_scr[h, lt] = jnp.broadcast_to(0.5 * (c16 + c17), (SUBLANES, LANES))
            s1 = s_scr[2 * h, lt]
            s2 = s_scr[2 * h + 1, lt]
            p_scr[2 * h, lt] = jnp.where(s1 >= v1[PEER_TOPK - 1:], jnp.exp(s1 - v1[0:1]), 0.0) / z
            p_scr[2 * h + 1, lt] = jnp.where(s2 >= v2[PEER_TOPK - 1:], jnp.exp(s2 - v2[0:1]), 0.0)
            return carry

        lax.fori_loop(0, PEER_HEADS * nlt, gate_body, 0)

    e1_rows = pl.ds(pl.multiple_of(jnp.clip(j - 1, 0, nblk - 1) * nrow, nrow), nrow)
    half_keys = PEER_NKEYS // 2

    def stages(act_new, act_cur, g_new, g_cur, first=True, middle=True, final=True):
        def pre_activation(rows):
            act_new[rows, :] = _dot(u_ref[rows, :], hnt_scr[...])

        def accumulate(rows):
            acc_scr[rows, :] += _dot(vt_ref[rows, :], g_cur[...])

        def gated(lt, r0, kh):
            cols = slice(lt * LANES, (lt + 1) * LANES)
            keys = slice(kh * half_keys, (kh + 1) * half_keys)
            ws = [jnp.zeros((half_keys, LANES), F32) for _ in range(2)]
            for h in range(PEER_HEADS):
                s2 = s_scr[2 * h + 1, lt, keys, :]
                p2 = p_scr[2 * h + 1, lt, keys, :]
                s1grp = s_scr[2 * h, lt, e1_rows, :]
                p1grp = p_scr[2 * h, lt, e1_rows, :]
                tau = tau_scr[h, lt][0:1]
                for d in range(2):
                    theta = tau - s1grp[r0 + d:r0 + d + 1]
                    ws[d] = ws[d] + jnp.where(s2 >= theta, p2, 0.0) * p1grp[r0 + d:r0 + d + 1]
            for d in range(2):
                rows = slice((r0 + d) * PEER_NKEYS + kh * half_keys, (r0 + d) * PEER_NKEYS + (kh + 1) * half_keys)
                g_new[rows, cols] = (ws[d] * _gelu_exact(act_cur[rows, cols])).astype(BF16)

        chunk = 4 * LANES
        matmul_jobs = []
        if first:
            matmul_jobs += [functools.partial(pre_activation, slice(m, m + chunk)) for m in range(0, eb, chunk)]
        if final:
            matmul_jobs += [functools.partial(accumulate, slice(m, m + chunk)) for m in range(0, D_MODEL, chunk)]
        gated_jobs = []
        if middle:
            gated_jobs = [functools.partial(gated, lt, r0, kh)
                          for lt in range(nlt) for r0 in range(0, nrow, 2) for kh in range(2)]
        per = -(-len(gated_jobs) // len(matmul_jobs))
        for k, job in enumerate(matmul_jobs):
            job()
            for gjob in gated_jobs[k * per:(k + 1) * per]:
                gjob()

    nstep = nblk + 2
    assert nblk % 2 == 0 and nblk >= 4
    steady = (j >= 2) & (j < nblk)

    @pl.when(j == 0)
    def _():
        stages(act0, act1, g0, g1, middle=False, final=False)

    @pl.when(j == 1)
    def _():
        stages(act1, act0, g1, g0, final=False)

    @pl.when(steady & (j % 2 == 0))
    def _():
        stages(act0, act1, g0, g1)

    @pl.when(steady & (j % 2 == 1))
    def _():
        stages(act1, act0, g1, g0)

    @pl.when(j == nstep - 2)
    def _():
        stages(act0, act1, g0, g1, first=False)

    @pl.when(j == nstep - 1)
    def _():
        stages(act1, act0, g1, g0, first=False, middle=False)

    def finish(y_ref):
        xo = x_ref[...] + acc_scr[...].T
        ms = jnp.mean(xo * xo, axis=-1, keepdims=True)
        y_ref[...] = xo * lax.rsqrt(ms + EPS) * fn_ref[...]

    last = j == pl.num_programs(1) - 1

    @pl.when(last & (i < n_first))
    def _():
        finish(ya_ref)

    @pl.when(last & (i >= n_first))
    def _():
        finish(yb_ref)


def _peer(x, n_first_rows, norm2_w, final_norm_w, wq, k1, k2, u_tab, v_tab, tm, eb):
    t = x.shape[0]
    assert t % tm == 0 and n_first_rows % tm == 0
    n_first = n_first_rows // tm
    nexp = u_tab.shape[0]
    assert nexp == PEER_NKEYS * PEER_NKEYS and nexp % eb == 0
    nblk = nexp // eb
    qw = 2 * PEER_HEADS * PEER_HALF
    nlt = tm // LANES
    full = lambda shape: pl.BlockSpec(shape, lambda i, j: (0,) * len(shape))
    out_a, out_b = _two_source_specs(tm, n_first)
    return pl.pallas_call(
        functools.partial(_peer_body, tm=tm, eb=eb, n_first=n_first, nblk=nblk),
        grid=(t // tm, nblk + 2),
        in_specs=[
            pl.BlockSpec((tm, D_MODEL), lambda i, j: (i, 0)),
            full((1, D_MODEL)), full((1, D_MODEL)),
            full((D_MODEL, qw)),
            full((PEER_HEADS, PEER_NKEYS, PEER_HALF)), full((PEER_HEADS, PEER_NKEYS, PEER_HALF)),
            pl.BlockSpec((eb, D_MODEL), lambda i, j: (jnp.minimum(j, nblk - 1), 0)),
            pl.BlockSpec((None, D_MODEL, eb), lambda i, j: (jnp.clip(j - 2, 0, nblk - 1), 0, 0)),
        ],
        out_specs=[out_a, out_b],
        out_shape=[jax.ShapeDtypeStruct((n_first_rows, D_MODEL), F32),
                   jax.ShapeDtypeStruct((t - n_first_rows, D_MODEL), F32)],
        scratch_shapes=[
            pltpu.VMEM((D_MODEL, tm), BF16),
            pltpu.VMEM((2 * PEER_HEADS, nlt, PEER_NKEYS, LANES), F32),
            pltpu.VMEM((2 * PEER_HEADS, nlt, PEER_NKEYS, LANES), F32),
            pltpu.VMEM((PEER_HEADS, nlt, SUBLANES, LANES), F32),
            pltpu.VMEM((2 * PEER_HEADS, nlt, PEER_TOPK, LANES), F32),
            pltpu.VMEM((eb, tm), F32), pltpu.VMEM((eb, tm), F32),
            pltpu.VMEM((eb, tm), BF16), pltpu.VMEM((eb, tm), BF16),
            pltpu.VMEM((D_MODEL, tm), F32),
        ],
        compiler_params=_params("arbitrary", "arbitrary"),
        name="peer",
    )(x, norm2_w.reshape(1, D_MODEL), final_norm_w.reshape(1, D_MODEL), wq.astype(BF16),
      k1.astype(BF16), k2.astype(BF16), u_tab.astype(BF16),
      jnp.transpose(v_tab.astype(BF16).reshape(nblk, eb, D_MODEL), (0, 2, 1)))


def kernel(x_prompt, x_sample, state_hgrn, state_ssm_re, state_ssm_im, meta_tokens, lower_bounds,
           norm1_w, w_in, g_norm_w, ssm_a_re, ssm_a_im, ssm_log_step, ssm_b_re, ssm_b_im,
           ssm_c_re, ssm_c_im, ssm_d, w_glu, b_glu, w_branch_a, w_branch_b, w_out, norm2_w,
           peer_wq, peer_k1, peer_k2, peer_u, peer_v, final_norm_w):
    depth = w_in.shape[0]
    assert depth == 1, "single-layer step only"
    nb, seq, _ = x_prompt.shape
    ns, dseq, _ = x_sample.shape
    tp = nb * seq
    tsm = ns * dseq
    tm = 512
    tm_in = 256
    hg_c = 128
    small_c = 16
    s5_tok = 8
    s5_seq = 4
    assert tp % tm == 0 and tsm % tm == 0 and tm % tm_in == 0 and seq % hg_c == 0 and seq % s5_tok == 0
    assert nb % s5_seq == 0 and dseq <= small_c and dseq % 2 == 0 and N_META % s5_tok == 0

    w_in_b = w_in[0].astype(BF16)
    xp = x_prompt.reshape(tp, D_MODEL).astype(F32)
    xs = x_sample.reshape(tsm, D_MODEL).astype(F32)
    proj, lg = _inproj(xp, xs, norm1_w[0], lower_bounds, w_in_b, tm_in)
    proj_m, lg_m = _inproj(meta_tokens.astype(F32), None, norm1_w[0], lower_bounds, w_in_b, N_META)

    zero_hg = jnp.zeros((1, HG_HEADS, HG_DIM, HG_DIM), F32)
    _, hg_meta = _hgrn(proj_m, lg_m, zero_hg, 1, 1, N_META, 0)
    o_p, hg_p = _hgrn(proj, lg, hg_meta, nb, seq // hg_c, hg_c, 0)
    o_s, hg_s = _hgrn(proj, lg, state_hgrn[0].astype(F32), ns, 1, small_c, tp, ntok=dseq)
    o_s = o_s.reshape(tsm, D_MODEL)

    ssm = (ssm_a_re[0], ssm_a_im[0], ssm_log_step[0], ssm_b_re[0], ssm_b_im[0], ssm_c_re[0],
           ssm_c_im[0], ssm_d[0])
    compact = _s5_pieces(*ssm, max(s5_tok, dseq))
    ops_p = _s5_operators(compact, s5_tok)
    ops_s = _s5_operators(compact, dseq)
    nst = SSM_G * SSM_P
    u_meta = jnp.pad(proj_m[4], ((0, 7 * N_META), (0, 0)))
    zero_ss = jnp.zeros((1, SUBLANES, nst), F32)
    _, mre, mim = _s5(u_meta, 0, ops_p, zero_ss, zero_ss, 0, 1, SUBLANES, N_META // s5_tok, s5_tok, 3)
    nblk = nb // s5_seq
    x0re = jnp.broadcast_to(mre[:, 0:1], (nblk, SUBLANES, nst))
    x0im = jnp.broadcast_to(mim[:, 0:1], (nblk, SUBLANES, nst))
    y_p, pre, pim = _s5(proj, 4, ops_p, x0re, x0im, 0, nblk, s5_seq, seq // s5_tok, s5_tok, 1)
    y_s, sre, sim = _s5(proj, 4, ops_s, state_ssm_re[0].astype(F32).reshape(1, ns, nst),
                        state_ssm_im[0].astype(F32).reshape(1, ns, nst), tp, 1, ns, 1, dseq, 3)

    x1 = _post(xp, xs, o_p, o_s, y_p, y_s, proj, g_norm_w[0], w_glu[0], b_glu[0], w_branch_a[0],
               w_branch_b[0], w_out[0], tm)
    y_pr, y_sm = _peer(x1, tp, norm2_w[0], final_norm_w, peer_wq[0], peer_k1[0], peer_k2[0],
                       peer_u[0], peer_v[0], tm, SUBLANES * PEER_NKEYS)

    sd = state_hgrn.dtype
    st = lambda a: a[:, :s5_seq].reshape(nb, SSM_G, SSM_P)[None]
    return (y_pr.reshape(nb, seq, D_MODEL).astype(x_prompt.dtype),
            y_sm.reshape(ns, dseq, D_MODEL).astype(x_sample.dtype),
            hg_p[None].astype(sd),
            st(pre).astype(state_ssm_re.dtype),
            st(pim).astype(state_ssm_im.dtype),
            hg_s[None].astype(sd),
            sre.reshape(1, ns, SSM_G, SSM_P).astype(state_ssm_re.dtype),
            sim.reshape(1, ns, SSM_G, SSM_P).astype(state_ssm_im.dtype))
```
